```python
import math
import jax, jax.numpy as jnp
from jax import lax
import numpy as np

D_MODEL = 1024
BATCH = 4
SEQ = 4096
DEPTH = 2

GDN_HEADS = 8
GDN_HEAD_DIM = 128
GDN_WIDTH = GDN_HEADS * GDN_HEAD_DIM
CONV_K = 4
GDN_CHUNK = 64
DSA_HEADS = 8
DSA_HEAD_DIM = 128
DSA_WIDTH = DSA_HEADS * DSA_HEAD_DIM
IDX_HEADS = 8
IDX_HEAD_DIM = 64
TOPK_MAX = 256
DSA_QBLK = 64
ROPE_THETA = 10000.0
FF_DIM = -(-8 * D_MODEL // (3 * 256)) * 256
NORM_EPS = 1e-6

IN_SIZES = (
    3 * GDN_WIDTH,
    GDN_WIDTH,
    GDN_HEADS,
    GDN_HEADS,
    3 * DSA_WIDTH,
    IDX_HEADS * IDX_HEAD_DIM,
    IDX_HEAD_DIM,
    IDX_HEADS,
    D_MODEL,
    D_MODEL,
)
IN_COLS = 3 * GDN_WIDTH + GDN_WIDTH + 2 * GDN_HEADS + 3 * DSA_WIDTH + IDX_HEADS * IDX_HEAD_DIM + IDX_HEAD_DIM + IDX_HEADS + 2 * D_MODEL

kernel_name = "hybrid_gdn_dsa_gated_block"


def rms_norm(x, gain):
    xf = x.astype(jnp.float32)
    y = xf * lax.rsqrt(jnp.mean(xf * xf, axis=-1, keepdims=True) + NORM_EPS)
    return (y * gain.astype(jnp.float32)).astype(x.dtype)


def l2norm(x):
    return x * lax.rsqrt(jnp.sum(x * x, axis=-1, keepdims=True) + NORM_EPS)


def split_cols(t, sizes):
    out, start = [], 0
    for s in sizes:
        out.append(t[..., start:start + s])
        start += s
    return out


def rope_tables(seq_len, dim):
    inv_freq = 1.0 / (ROPE_THETA ** (jnp.arange(0, dim, 2, dtype=jnp.float32) / dim))
    ang = jnp.arange(seq_len, dtype=jnp.float32)[:, None] * inv_freq[None, :]
    return jnp.cos(ang), jnp.sin(ang)


def apply_rope(x, cos, sin):
    half = x.shape[-1] // 2
    extra = x.ndim - 3
    c = cos.reshape(cos.shape[0], *([1] * extra), half)
    s = sin.reshape(sin.shape[0], *([1] * extra), half)
    x1 = x[..., :half].astype(jnp.float32)
    x2 = x[..., half:].astype(jnp.float32)
    return jnp.concatenate([x1 * c - x2 * s, x2 * c + x1 * s], axis=-1).astype(x.dtype)


def causal_short_conv(x, w):
    L = x.shape[1]
    xp = jnp.pad(x, ((0, 0), (CONV_K - 1, 0), (0, 0)))
    y = w[0] * xp[:, 0:L]
    for j in range(1, CONV_K):
        y = y + w[j] * xp[:, j:j + L]
    return jax.nn.silu(y)


def chunk_gated_delta_rule(q, k, v, g, beta):
    B, L, H, dk = q.shape
    dv = v.shape[-1]
    C = GDN_CHUNK
    N = L // C
    f32 = jnp.float32

    def chunks(t):
        return jnp.moveaxis(t.astype(f32).reshape(B, N, C, H, *t.shape[3:]), 3, 1)

    q = chunks(q) * (dk ** -0.5)
    k = chunks(k)
    v = chunks(v)
    gam = jnp.cumsum(chunks(g), axis=-1)
    beta = chunks(beta)

    tril_incl = jnp.tril(jnp.ones((C, C), dtype=bool))
    tril_strict = jnp.tril(jnp.ones((C, C), dtype=bool), -1)
    decay = jnp.exp(jnp.where(tril_incl, gam[..., :, None] - gam[..., None, :], -jnp.inf))

    kk = jnp.einsum('bhnid,bhnjd->bhnij', k, k)
    a_mat = jnp.where(tril_strict, beta[..., None] * kk * decay, 0.0)
    lhs = jnp.eye(C, dtype=f32) + a_mat
    rhs = jnp.concatenate([v * beta[..., None], k * (beta * jnp.exp(gam))[..., None]], axis=-1)
    sol = lax.linalg.triangular_solve(lhs, rhs, left_side=True, lower=True, unit_diagonal=True)
    u, w = sol[..., :dv], sol[..., dv:]

    qk = jnp.where(tril_incl, jnp.einsum('bhnid,bhnjd->bhnij', q, k) * decay, 0.0)
    q_dec = q * jnp.exp(gam)[..., None]
    k_dec = k * jnp.exp(gam[..., -1:] - gam)[..., None]
    g_last = jnp.exp(gam[..., -1])

    def step(S, xs):
        u_c, w_c, qk_c, qd_c, kd_c, gl_c = xs
        v_new = u_c - jnp.einsum('bhcd,bhde->bhce', w_c, S)
        o_c = jnp.einsum('bhcd,bhde->bhce', qd_c, S) + jnp.einsum('bhij,bhje->bhie', qk_c, v_new)
        S = S * gl_c[..., None, None] + jnp.einsum('bhcd,bhce->bhde', kd_c, v_new)
        return S, o_c

    xs = tuple(jnp.moveaxis(t, 2, 0) for t in (u, w, qk, q_dec, k_dec, g_last))
    S0 = jnp.zeros((B, H, dk, dv), dtype=f32)
    _, o = lax.scan(step, S0, xs)
    o = jnp.moveaxis(o, 0, 2).reshape(B, H, L, dv)
    return jnp.transpose(o, (0, 2, 1, 3))


def dsa_attention(q, k, v, q_idx, k_idx, w_idx):
    B, L, H, dh = q.shape
    topk = min(TOPK_MAX, L // 4)
    nblk = L // DSA_QBLK
    key_pos = jnp.arange(L, dtype=jnp.int32)
    k_idx_f = k_idx.astype(jnp.float32) * (IDX_HEAD_DIM ** -0.5)
    gather = jax.vmap(lambda kk, ii: kk[ii])

    def blocks(t):
        return jnp.moveaxis(t.reshape(B, nblk, DSA_QBLK, *t.shape[2:]), 1, 0)

    def one_block(args):
        qb, qib, wb, start = args
        qpos = start + jnp.arange(DSA_QBLK, dtype=jnp.int32)
        logits = jnp.einsum('bqhd,bsd->bqhs', qib.astype(jnp.float32), k_idx_f)
        score = jnp.einsum('bqh,bqhs->bqs', wb.astype(jnp.float32) * (IDX_HEADS ** -0.5), jax.nn.relu(logits))
        causal = key_pos[None, :] <= qpos[:, None]
        score = jnp.where(causal[None], score, -jnp.inf)
        _, idx = lax.top_k(score, topk)
        ks = gather(k, idx).astype(jnp.float32)
        vs = gather(v, idx).astype(jnp.float32)
        s = jnp.einsum('bqhd,bqkhd->bqhk', qb.astype(jnp.float32), ks) * (dh ** -0.5)
        valid = idx <= qpos[None, :, None]
        s = jnp.where(valid[:, :, None, :], s, -jnp.inf)
        p = jax.nn.softmax(s, axis=-1)
        return jnp.einsum('bqhk,bqkhd->bqhd', p, vs).astype(q.dtype)

    starts = jnp.arange(nblk, dtype=jnp.int32) * DSA_QBLK
    o = lax.map(one_block, (blocks(q), blocks(q_idx), blocks(w_idx), starts))
    return jnp.moveaxis(o, 0, 1).reshape(B, L, H * dh)


def hybrid_mixer(h, w_in, conv_w, a_log, dt_bias, gdn_norm, w_out_gdn, w_out_dsa, w_o, cos_a, sin_a, cos_i, sin_i):
    B, L, _ = h.shape
    proj = h @ w_in
    (qkv_a, z_a, a_a, b_a, qkv_b, q_i, k_i, w_i, g_a, g_b) = split_cols(proj, IN_SIZES)

    qkv_a = causal_short_conv(qkv_a, conv_w)
    q_a, k_a, v_a = split_cols(qkv_a, (GDN_WIDTH, GDN_WIDTH, GDN_WIDTH))
    q_a = l2norm(q_a.reshape(B, L, GDN_HEADS, GDN_HEAD_DIM).astype(jnp.float32))
    k_a = l2norm(k_a.reshape(B, L, GDN_HEADS, GDN_HEAD_DIM).astype(jnp.float32))
    v_a = v_a.reshape(B, L, GDN_HEADS, GDN_HEAD_DIM)
    g = -jnp.exp(a_log.astype(jnp.float32)) * jax.nn.softplus(a_a.astype(jnp.float32) + dt_bias.astype(jnp.float32))
    beta = jax.nn.sigmoid(b_a.astype(jnp.float32))
    o_a = chunk_gated_delta_rule(q_a, k_a, v_a, g, beta)
    o_a = rms_norm(o_a, gdn_norm) * jax.nn.silu(z_a.reshape(B, L, GDN_HEADS, GDN_HEAD_DIM).astype(jnp.float32))
    y_a = o_a.reshape(B, L, GDN_WIDTH).astype(h.dtype) @ w_out_gdn

    q_b, k_b, v_b = split_cols(qkv_b, (DSA_WIDTH, DSA_WIDTH, DSA_WIDTH))
    q_b = apply_rope(q_b.reshape(B, L, DSA_HEADS, DSA_HEAD_DIM), cos_a, sin_a)
    k_b = apply_rope(k_b.reshape(B, L, DSA_HEADS, DSA_HEAD_DIM), cos_a, sin_a)
    v_b = v_b.reshape(B, L, DSA_HEADS, DSA_HEAD_DIM)
    q_i = apply_rope(q_i.reshape(B, L, IDX_HEADS, IDX_HEAD_DIM), cos_i, sin_i)
    k_i = apply_rope(k_i, cos_i, sin_i)
    y_b = dsa_attention(q_b, k_b, v_b, q_i, k_i, w_i) @ w_out_dsa

    merged = jax.nn.sigmoid(g_a) * y_a + jax.nn.sigmoid(g_b) * y_b
    return merged @ w_o


def swiglu(h, w_gate_up, w_down):
    gu = h @ w_gate_up
    return (jax.nn.silu(gu[..., :FF_DIM]) * gu[..., FF_DIM:]) @ w_down


def setup_inputs(seed: int = 0) -> dict:
    key = jax.random.key(seed)
    ks = jax.random.split(key, 15)
    f32 = jnp.float32

    def normal(k, shape, scale):
        return jax.random.normal(k, shape, f32) * scale

    def gain(k, shape):
        return 1.0 + 0.02 * jax.random.normal(k, shape, f32)

    x = normal(ks[0], (BATCH, SEQ, D_MODEL), 1.0)
    norm_mix = gain(ks[1], (DEPTH, D_MODEL))
    w_in = normal(ks[2], (DEPTH, D_MODEL, IN_COLS), D_MODEL ** -0.5)
    conv_w = normal(ks[3], (DEPTH, CONV_K, 3 * GDN_WIDTH), CONV_K ** -0.5)
    a_log = jnp.log(jax.random.uniform(ks[4], (DEPTH, GDN_HEADS), f32, minval=1.0, maxval=16.0))
    dt = jnp.exp(jax.random.uniform(ks[5], (DEPTH, GDN_HEADS), f32, minval=math.log(1e-3), maxval=math.log(1e-1)))
    dt_bias = dt + jnp.log(-jnp.expm1(-dt))
    gdn_norm = gain(ks[6], (DEPTH, GDN_HEAD_DIM))
    w_out_gdn = normal(ks[7], (DEPTH, GDN_WIDTH, D_MODEL), GDN_WIDTH ** -0.5)
    w_out_dsa = normal(ks[8], (DEPTH, DSA_WIDTH, D_MODEL), DSA_WIDTH ** -0.5)
    w_o = normal(ks[9], (DEPTH, D_MODEL, D_MODEL), D_MODEL ** -0.5)
    norm_ffn = gain(ks[10], (DEPTH, D_MODEL))
    w_gate_up = normal(ks[11], (DEPTH, D_MODEL, 2 * FF_DIM), D_MODEL ** -0.5)
    w_down = normal(ks[12], (DEPTH, FF_DIM, D_MODEL), FF_DIM ** -0.5)
    norm_final = gain(ks[13], (D_MODEL,))
    return {"x": x, "norm_mix": norm_mix, "w_in": w_in, "conv_w": conv_w, "a_log": a_log,
            "dt_bias": dt_bias, "gdn_norm": gdn_norm, "w_out_gdn": w_out_gdn, "w_out_dsa": w_out_dsa,
            "w_o": w_o, "norm_ffn": norm_ffn, "w_gate_up": w_gate_up, "w_down": w_down,
            "norm_final": norm_final}


def reference(x, norm_mix, w_in, conv_w, a_log, dt_bias, gdn_norm, w_out_gdn, w_out_dsa, w_o,
              norm_ffn, w_gate_up, w_down, norm_final):
    L = x.shape[1]
    cos_a, sin_a = rope_tables(L, DSA_HEAD_DIM)
    cos_i, sin_i = rope_tables(L, IDX_HEAD_DIM)
    h = x
    for l in range(DEPTH):
        h = h + hybrid_mixer(rms_norm(h, norm_mix[l]), w_in[l], conv_w[l], a_log[l], dt_bias[l],
                             gdn_norm[l], w_out_gdn[l], w_out_dsa[l], w_o[l], cos_a, sin_a, cos_i, sin_i)
        h = h + swiglu(rms_norm(h, norm_ffn[l]), w_gate_up[l], w_down[l])
    return rms_norm(h, norm_final)
```

```python
import functools
import math

import jax
import jax.numpy as jnp
from jax import lax
from jax.experimental import pallas as pl
from jax.experimental.pallas import tpu as pltpu

F32 = jnp.float32
BF16 = jnp.bfloat16
HIGHEST = lax.Precision.HIGHEST

GDN_HEADS = 8
GDN_HEAD_DIM = 128
GDN_WIDTH = GDN_HEADS * GDN_HEAD_DIM
CONV_K = 4
GDN_CHUNK = 64
DSA_HEADS = 8
DSA_HEAD_DIM = 128
DSA_WIDTH = DSA_HEADS * DSA_HEAD_DIM
IDX_HEADS = 8
IDX_HEAD_DIM = 64
TOPK_MAX = 256
ROPE_THETA = 10000.0
NORM_EPS = 1e-6

LANES = 128
VMEM_LIMIT = 48 * 1024 * 1024
BISECT_STEPS = 32
SELECT_ROW_BLOCK = 64
MASK_BIAS = -1e30

COL_QKV_A = 0
COL_Z_A = 3 * GDN_WIDTH
COL_QKV_B = COL_Z_A + GDN_WIDTH
COL_Q_I = COL_QKV_B + 3 * DSA_WIDTH
COL_G_A = COL_Q_I + IDX_HEADS * IDX_HEAD_DIM
MAIN_COLS = COL_G_A + 2 * 1024
SM_A, SM_B, SM_W, SM_K = 0, 8, 16, 64


def _cparams(*sem):
    return pltpu.CompilerParams(dimension_semantics=sem, vmem_limit_bytes=VMEM_LIMIT)


def _rmsnorm_rows(x, gain):
    ms = jnp.mean(x * x, axis=-1, keepdims=True)
    return x * lax.rsqrt(ms + NORM_EPS) * gain


def _dot(a, b):
    return jnp.dot(a, b, preferred_element_type=F32)


def _dot_nt(a, b):
    return lax.dot_general(a, b, (((1,), (1,)), ((), ())), preferred_element_type=F32)


def _resident(shape):
    return pl.BlockSpec(shape, lambda i, j: (0,) * len(shape), pipeline_mode=pl.Buffered(1))


def _norm_mm_body(x_ref, g_ref, w_ref, o_ref, xn_ref, *, tn, exact):
    @pl.when(pl.program_id(1) == 0)
    def _():
        xn_ref[...] = _rmsnorm_rows(x_ref[...], g_ref[...]).astype(xn_ref.dtype)

    w = w_ref[:, pl.ds(pl.multiple_of(pl.program_id(1) * tn, tn), tn)]
    if exact:
        o = jnp.dot(xn_ref[...], w, precision=HIGHEST, preferred_element_type=F32)
    else:
        o = _dot(xn_ref[...], w)
    o_ref[...] = o.astype(o_ref.dtype)


def _norm_matmul(x, gain, w, *, tm, tn, out_dtype, exact=False):
    m, k = x.shape
    n = w.shape[1]
    return pl.pallas_call(
        functools.partial(_norm_mm_body, tn=tn, exact=exact),
        grid=(m // tm, n // tn),
        in_specs=[pl.BlockSpec((tm, k), lambda i, j: (i, 0)),
                  _resident((1, k)),
                  _resident((k, n))],
        out_specs=pl.BlockSpec((tm, tn), lambda i, j: (i, j)),
        out_shape=jax.ShapeDtypeStruct((m, n), out_dtype),
        scratch_shapes=[pltpu.VMEM((tm, k), w.dtype)],
        compiler_params=_cparams("parallel", "arbitrary"),
        name="norm_matmul_exact" if exact else "norm_matmul",
    )(x, gain.reshape(1, k), w)


def _norm_swiglu_body(x_ref, g_ref, w_ref, o_ref, xn_ref, *, tn, f):
    @pl.when(pl.program_id(1) == 0)
    def _():
        xn_ref[...] = _rmsnorm_rows(x_ref[...], g_ref[...]).astype(xn_ref.dtype)

    xn = xn_ref[...]
    c0 = pl.multiple_of(pl.program_id(1) * tn, tn)
    g = _dot(xn, w_ref[:, pl.ds(c0, tn)])
    u = _dot(xn, w_ref[:, pl.ds(c0 + f, tn)])
    o_ref[...] = (g * jax.nn.sigmoid(g) * u).astype(o_ref.dtype)


def _norm_swiglu(x, gain, w_gate_up, *, tm, tn):
    m, k = x.shape
    f = w_gate_up.shape[1] // 2
    return pl.pallas_call(
        functools.partial(_norm_swiglu_body, tn=tn, f=f),
        grid=(m // tm, f // tn),
        in_specs=[pl.BlockSpec((tm, k), lambda i, j: (i, 0)),
                  _resident((1, k)),
                  _resident((k, 2 * f))],
        out_specs=pl.BlockSpec((tm, tn), lambda i, j: (i, j)),
        out_shape=jax.ShapeDtypeStruct((m, f), BF16),
        scratch_shapes=[pltpu.VMEM((tm, k), BF16)],
        compiler_params=_cparams("parallel", "arbitrary"),
        name="norm_swiglu",
    )(x, gain.reshape(1, k), w_gate_up)


def _mm_resid_body(x_ref, w_ref, r_ref, *rest, final_norm):
    if final_norm:
        g_ref, o_ref = rest
    else:
        (o_ref,) = rest
    h = r_ref[...] + _dot(x_ref[...], w_ref[...])
    if final_norm:
        h = _rmsnorm_rows(h, g_ref[...])
    o_ref[...] = h


def _matmul_resid(x, w, resid, *, tm, final_gain=None):
    m, k = x.shape
    n = w.shape[1]
    final_norm = final_gain is not None
    in_specs = [pl.BlockSpec((tm, k), lambda i: (i, 0)),
                pl.BlockSpec((k, n), lambda i: (0, 0)),
                pl.BlockSpec((tm, n), lambda i: (i, 0))]
    args = [x, w, resid]
    if final_norm:
        in_specs.append(pl.BlockSpec((1, n), lambda i: (0, 0)))
        args.append(final_gain.reshape(1, n))
    return pl.pallas_call(
        functools.partial(_mm_resid_body, final_norm=final_norm),
        grid=(m // tm,),
        in_specs=in_specs,
        out_specs=pl.BlockSpec((tm, n), lambda i: (i, 0)),
        out_shape=jax.ShapeDtypeStruct((m, n), F32),
        compiler_params=_cparams("parallel"),
        name="matmul_resid_norm" if final_norm else "matmul_resid",
    )(*args)


def _gated_merge_body(oa_ref, ob_ref, wa_ref, wb_ref, ga_ref, gb_ref, o_ref):
    ya = _dot(oa_ref[...], wa_ref[...])
    yb = _dot(ob_ref[...], wb_ref[...])
    ga = jax.nn.sigmoid(ga_ref[...].astype(F32))
    gb = jax.nn.sigmoid(gb_ref[...].astype(F32))
    o_ref[...] = (ga * ya + gb * yb).astype(o_ref.dtype)


def _gated_merge(oa, ob, wa, wb, proj, *, tm, tn):
    m, k = oa.shape
    n = wa.shape[1]
    ca, cb = COL_G_A // tn, (COL_G_A + n) // tn
    return pl.pallas_call(
        _gated_merge_body,
        grid=(m // tm, n // tn),
        in_specs=[pl.BlockSpec((tm, k), lambda i, j: (i, 0)),
                  pl.BlockSpec((tm, k), lambda i, j: (i, 0)),
                  pl.BlockSpec((k, tn), lambda i, j: (0, j)),
                  pl.BlockSpec((k, tn), lambda i, j: (0, j)),
                  pl.BlockSpec((tm, tn), lambda i, j: (i, ca + j)),
                  pl.BlockSpec((tm, tn), lambda i, j: (i, cb + j))],
        out_specs=pl.BlockSpec((tm, tn), lambda i, j: (i, j)),
        out_shape=jax.ShapeDtypeStruct((m, n), BF16),
        compiler_params=_cparams("parallel", "arbitrary"),
        name="gated_merge",
    )(oa, ob, wa, wb, proj, proj)


HALO = 16


def _gdn_prep_body(x_ref, halo_ref, cw_ref, q_ref, k_ref, v_ref, xs_ref, *, tl):
    i = pl.program_id(1)
    halo = halo_ref[0].astype(F32)
    xs_ref[0:HALO, :] = jnp.where(i == 0, 0.0, halo)
    xs_ref[HALO:HALO + tl, :] = x_ref[0].astype(F32)
    y = cw_ref[0:1, :] * xs_ref[pl.ds(HALO - 3, tl), :]
    for j in range(1, CONV_K):
        y = y + cw_ref[j:j + 1, :] * xs_ref[pl.ds(HALO - 3 + j, tl), :]
    y = y * jax.nn.sigmoid(y)
    for h in range(GDN_HEADS):
        lo = h * GDN_HEAD_DIM
        qh = y[:, lo:lo + GDN_HEAD_DIM]
        kh = y[:, GDN_WIDTH + lo:GDN_WIDTH + lo + GDN_HEAD_DIM]
        qn = qh * lax.rsqrt(jnp.sum(qh * qh, axis=-1, keepdims=True) + NORM_EPS)
        kn = kh * lax.rsqrt(jnp.sum(kh * kh, axis=-1, keepdims=True) + NORM_EPS)
        q_ref[0, :, lo:lo + GDN_HEAD_DIM] = qn * (GDN_HEAD_DIM ** -0.5)
        k_ref[0, :, lo:lo + GDN_HEAD_DIM] = kn
    v_ref[0] = y[:, 2 * GDN_WIDTH:]


def _gdn_prep(proj, conv_w, *, tl):
    b, l, _ = proj.shape
    c = 3 * GDN_WIDTH
    r = tl // HALO
    out = jax.ShapeDtypeStruct((b, l, GDN_WIDTH), F32)
    ospec = pl.BlockSpec((1, tl, GDN_WIDTH), lambda bi, i: (bi, i, 0))
    return pl.pallas_call(
        functools.partial(_gdn_prep_body, tl=tl),
        grid=(b, l // tl),
        in_specs=[pl.BlockSpec((1, tl, c), lambda bi, i: (bi, i, 0)),
                  pl.BlockSpec((1, HALO, c), lambda bi, i: (bi, jnp.maximum(i * r - 1, 0), 0)),
                  pl.BlockSpec((CONV_K, c), lambda bi, i: (0, 0))],
        out_specs=[ospec, ospec, ospec],
        out_shape=[out, out, out],
        scratch_shapes=[pltpu.VMEM((HALO + tl, c), F32)],
        compiler_params=_cparams("parallel", "parallel"),
        name="gdn_prep",
    )(proj, proj, conv_w)


def _gdn_chunk_body(q_ref, k_ref, v_ref, z_ref, sm_ref, alog_ref, dtb_ref, gn_ref, o_ref,
                    s_ref, wq_ref, qkk_ref, u_ref, gl_ref, *, nch):
    c = GDN_CHUNK
    d = GDN_HEAD_DIM
    heads = range(GDN_HEADS)

    @pl.when(pl.program_id(1) == 0)
    def _():
        s_ref[...] = jnp.zeros_like(s_ref)

    row = lax.broadcasted_iota(jnp.int32, (c, c), 0)
    col = lax.broadcasted_iota(jnp.int32, (c, c), 1)
    tril_incl = row >= col
    tril_strict = row > col
    eye = (row == col).astype(F32)
    tri_f = tril_incl.astype(F32)
    triu_f = (row <= col).astype(F32)

    def intra(ci, carry):
        r0 = pl.multiple_of(ci * c, c)
        sm = sm_ref[0, pl.ds(r0, c), :]
        g_all = -jnp.exp(alog_ref[...]) * jax.nn.softplus(sm + dtb_ref[...])
        beta_all = jax.nn.sigmoid(sm)
        gam_col = jnp.dot(tri_f, g_all, precision=HIGHEST, preferred_element_type=F32)
        gam_row = jnp.dot(g_all.T, triu_f, precision=HIGHEST, preferred_element_type=F32)
        gc = [gam_col[:, SM_A + h:SM_A + h + 1] for h in heads]
        bc = [beta_all[:, SM_B + h:SM_B + h + 1] for h in heads]
        g_last = [g[c - 1:c, :] for g in gc]
        decay = [jnp.exp(jnp.minimum(gc[h] - gam_row[SM_A + h:SM_A + h + 1, :], 0.0)) for h in heads]
        qh = [q_ref[0, pl.ds(r0, c), h * d:(h + 1) * d] for h in heads]
        kh = [k_ref[0, pl.ds(r0, c), h * d:(h + 1) * d] for h in heads]
        vh = [v_ref[0, pl.ds(r0, c), h * d:(h + 1) * d] for h in heads]
        kq = [_dot_nt(jnp.concatenate([kh[h], qh[h]], axis=0).astype(BF16), kh[h].astype(BF16))
              for h in heads]
        qk = [jnp.where(tril_incl, kq[h][c:] * decay[h], 0.0) for h in heads]
        n_pow = [-jnp.where(tril_strict, bc[h] * kq[h][:c] * decay[h], 0.0) for h in heads]
        t_inv = [eye + n for n in n_pow]
        n_pow = [_dot(n.astype(BF16), n.astype(BF16)) for n in n_pow]
        for _ in range(int(math.log2(c)) - 2):
            prod = [_dot(jnp.concatenate([n, t], axis=0).astype(BF16), n.astype(BF16))
                    for n, t in zip(n_pow, t_inv)]
            n_pow = [p[:c] for p in prod]
            t_inv = [t + p[c:] for t, p in zip(t_inv, prod)]
        t_inv = [t + _dot(t.astype(BF16), n.astype(BF16)) for n, t in zip(n_pow, t_inv)]
        e_gc = [jnp.exp(g) for g in gc]
        rhs = [jnp.concatenate([vh[h] * bc[h], kh[h] * (bc[h] * e_gc[h])], axis=1) for h in heads]
        uw = [_dot(t_inv[h].astype(BF16), rhs[h].astype(BF16)) for h in heads]
        for h in heads:
            k_dec = kh[h] * jnp.exp(g_last[h] - gc[h])
            u_ref[ci, h] = uw[h][:, :d]
            wq_ref[ci, h] = jnp.concatenate([uw[h][:, d:], qh[h] * e_gc[h]], axis=0).astype(BF16)
            qkk_ref[ci, h] = jnp.concatenate([qk[h], k_dec.T], axis=0).astype(BF16)
            gl_ref[ci, h] = jnp.broadcast_to(jnp.exp(g_last[h]), (1, d))
        return carry

    lax.fori_loop(0, nch, intra, 0)

    def scan(ci, carry):
        r0 = pl.multiple_of(ci * c, c)
        s_old = [s_ref[h] for h in heads]
        m1 = [_dot(wq_ref[ci, h], s_old[h].astype(BF16)) for h in heads]
        v_new = [u_ref[ci, h] - m1[h][:c] for h in heads]
        m2 = [_dot(qkk_ref[ci, h], v_new[h].astype(BF16)) for h in heads]
        for h in heads:
            s_ref[h] = s_old[h] * gl_ref[ci, h] + m2[h][c:]
            o = m1[h][c:] + m2[h][:c]
            o = o * lax.rsqrt(jnp.mean(o * o, axis=-1, keepdims=True) + NORM_EPS) * gn_ref[...]
            z = z_ref[0, pl.ds(r0, c), h * d:(h + 1) * d].astype(F32)
            o_ref[0, pl.ds(r0, c), h * d:(h + 1) * d] = (o * (z * jax.nn.sigmoid(z))).astype(o_ref.dtype)
        return carry

    lax.fori_loop(0, nch, scan, 0)


def _gdn_chunk(q, k, v, proj, smalls, a_log, dt_bias, gdn_norm, *, tl):
    b, l, _ = q.shape
    c, d, nh = GDN_CHUNK, GDN_HEAD_DIM, GDN_HEADS
    nch = tl // c
    pad = lambda t, at: jnp.zeros((1, LANES), F32).at[0, at:at + t.shape[0]].set(t.astype(F32))
    qspec = pl.BlockSpec((1, tl, GDN_WIDTH), lambda bi, i: (bi, i, 0))
    vec = pl.BlockSpec((1, LANES), lambda bi, i: (0, 0))
    return pl.pallas_call(
        functools.partial(_gdn_chunk_body, nch=nch),
        grid=(b, l // tl),
        in_specs=[qspec, qspec, qspec,
                  pl.BlockSpec((1, tl, GDN_WIDTH), lambda bi, i: (bi, i, COL_Z_A // GDN_WIDTH)),
                  pl.BlockSpec((1, tl, LANES), lambda bi, i: (bi, i, 0)),
                  vec, vec, vec],
        out_specs=qspec,
        out_shape=jax.ShapeDtypeStruct((b, l, GDN_WIDTH), BF16),
        scratch_shapes=[pltpu.VMEM((nh, d, d), F32),
                        pltpu.VMEM((nch, nh, 2 * c, d), BF16),
                        pltpu.VMEM((nch, nh, c + d, c), BF16),
                        pltpu.VMEM((nch, nh, c, d), F32),
                        pltpu.VMEM((nch, nh, 1, d), F32)],
        compiler_params=_cparams("parallel", "arbitrary"),
        name="gdn_chunk",
    )(q, k, v, proj, smalls, pad(a_log, SM_A), pad(dt_bias, SM_A), gdn_norm.reshape(1, LANES).astype(F32))


def _swap_halves(x, half):
    if 2 * half == LANES:
        return pltpu.roll(x, half, axis=1)
    lane = lax.broadcasted_iota(jnp.int32, x.shape, 1)
    first = (lane % (2 * half)) < half
    return jnp.where(first, pltpu.roll(x, LANES - half, axis=1), pltpu.roll(x, half, axis=1))


def _dsa_prep_body(qk_ref, qi_ref, sm_ref, ca_ref, sa_ref, ci_ref, si_ref,
                   qb_ref, kb_ref, qio_ref, kio_ref):
    ca, sa = ca_ref[...], sa_ref[...]
    for h in range(2 * DSA_HEADS):
        lo = h * DSA_HEAD_DIM
        x = qk_ref[0, :, lo:lo + DSA_HEAD_DIM].astype(F32)
        y = x * ca + _swap_halves(x, DSA_HEAD_DIM // 2) * sa
        if h < DSA_HEADS:
            qb_ref[0, :, lo:lo + DSA_HEAD_DIM] = (y * (DSA_HEAD_DIM ** -0.5)).astype(BF16)
        else:
            lo -= DSA_WIDTH
            kb_ref[0, :, lo:lo + DSA_HEAD_DIM] = y.astype(BF16)
    ci, si = ci_ref[...], si_ref[...]
    for p in range(IDX_HEADS * IDX_HEAD_DIM // LANES):
        lo = p * LANES
        x = qi_ref[0, :, lo:lo + LANES].astype(F32)
        y = x * ci + _swap_halves(x, IDX_HEAD_DIM // 2) * si
        qio_ref[0, :, lo:lo + LANES] = y.astype(BF16)
    x = sm_ref[0]
    y = x * ci + _swap_halves(x, IDX_HEAD_DIM // 2) * si
    kio_ref[0] = (y[:, SM_K:] * (IDX_HEAD_DIM ** -0.5)).astype(BF16)


def _dsa_prep(proj, smalls, tabs, *, tl):
    b, l, _ = proj.shape
    w2 = 2 * DSA_WIDTH
    wi = IDX_HEADS * IDX_HEAD_DIM
    tab = pl.BlockSpec((tl, LANES), lambda bi, i: (i, 0))
    big = lambda n: pl.BlockSpec((1, tl, n), lambda bi, i: (bi, i, 0))
    return pl.pallas_call(
        _dsa_prep_body,
        grid=(b, l // tl),
        in_specs=[pl.BlockSpec((1, tl, w2), lambda bi, i: (bi, i, COL_QKV_B // w2)),
                  pl.BlockSpec((1, tl, wi), lambda bi, i: (bi, i, COL_Q_I // wi)),
                  big(LANES), tab, tab, tab, tab],
        out_specs=[big(DSA_WIDTH), big(DSA_WIDTH), big(wi), big(IDX_HEAD_DIM)],
        out_shape=[jax.ShapeDtypeStruct((b, l, DSA_WIDTH), BF16),
                   jax.ShapeDtypeStruct((b, l, DSA_WIDTH), BF16),
                   jax.ShapeDtypeStruct((b, l, wi), BF16),
                   jax.ShapeDtypeStruct((b, l, IDX_HEAD_DIM), BF16)],
        compiler_params=_cparams("parallel", "parallel"),
        name="dsa_prep",
    )(proj, proj, smalls, *tabs)


def _dsa_select_body(qi_ref, sm_ref, ki_ref, o_ref, s_ref, *, tq, tc, topk):
    l = ki_ref.shape[1]
    nchunks_total = l // tc
    nl = tc // LANES
    q0 = pl.program_id(1) * tq
    n_c = (q0 + tq + tc - 1) // tc
    kf = float(topk)
    w_all = sm_ref[0] * (IDX_HEADS ** -0.5)
    qpos = q0 + lax.broadcasted_iota(jnp.int32, (tq, tc), 0)
    kloc = lax.broadcasted_iota(jnp.int32, (tq, tc), 1)
    rb = min(SELECT_ROW_BLOCK, tq)
    rep = lambda col: jnp.broadcast_to(col, (col.shape[0], LANES))
    tiles = lambda x: [x[:, j * LANES:(j + 1) * LANES] for j in range(nl)]

    def score_chunk(ci, carry):
        mn, mx, n_pos, n_nonneg = carry
        k0 = pl.multiple_of(ci * tc, tc)
        kc = ki_ref[0, pl.ds(k0, tc), :]
        s = jnp.zeros((tq, tc), F32)
        for h in range(IDX_HEADS):
            qh = qi_ref[0, :, h * IDX_HEAD_DIM:(h + 1) * IDX_HEAD_DIM]
            s = s + w_all[:, SM_W + h:SM_W + h + 1] * jnp.maximum(_dot_nt(qh, kc), 0.0)
        causal = k0 + kloc <= qpos
        s_lo = jnp.where(causal, s, -jnp.inf)
        s_ref[:, pl.ds(k0, tc)] = s_lo
        for lo_t, hi_t in zip(tiles(s_lo), tiles(jnp.where(causal, s, jnp.inf))):
            mx = jnp.maximum(mx, lo_t)
            mn = jnp.minimum(mn, hi_t)
            n_pos = n_pos + jnp.where(lo_t > 0.0, 1.0, 0.0)
            n_nonneg = n_nonneg + jnp.where(lo_t >= 0.0, 1.0, 0.0)
        return mn, mx, n_pos, n_nonneg

    zeros = jnp.zeros((tq, LANES), F32)
    mn, mx, n_pos, n_nonneg = lax.fori_loop(0, n_c, score_chunk,
                                            (zeros + jnp.inf, zeros - jnp.inf, zeros, zeros))
    row_min = rep(jnp.min(mn, axis=1, keepdims=True))
    row_max = rep(jnp.max(mx, axis=1, keepdims=True))
    n_pos = rep(jnp.sum(n_pos, axis=1, keepdims=True))
    n_nonneg = rep(jnp.sum(n_nonneg, axis=1, keepdims=True))

    def reduce_keys(fn, init, combine, finish):
        accs = []
        for r in range(tq // rb):
            rows = slice(r * rb, (r + 1) * rb)

            def body(ci, acc, rows=rows):
                k0 = pl.multiple_of(ci * tc, tc)
                for t in tiles(s_ref[rows, pl.ds(k0, tc)]):
                    acc = combine(acc, fn(t, rows))
                return acc
            accs.append(lax.fori_loop(0, n_c, body, jnp.full((rb, LANES), init, F32)))
        return rep(finish(jnp.concatenate(accs, axis=0), axis=1, keepdims=True))

    def count(op, th):
        return reduce_keys(lambda t, rows: jnp.where(op(t, th[rows]), 1.0, 0.0), 0.0, jnp.add, jnp.sum)

    ge = lambda a, b: a >= b
    any_open = lambda lo, hi: (jnp.max(hi - lo) > 0.0).astype(jnp.int32)

    n_causal = (q0 + 1 + lax.broadcasted_iota(jnp.int32, (tq, LANES), 0)).astype(F32)
    few = n_causal <= kf
    starts_closed = few | ((n_pos < kf) & (n_nonneg >= kf))
    closed_at = jnp.where(few, -jnp.finfo(F32).max, 0.0)
    positive = n_pos >= kf
    lo = jnp.where(starts_closed, closed_at, jnp.where(positive, jnp.maximum(row_min, 0.0), row_min))
    hi = jnp.where(starts_closed, closed_at, jnp.where(positive, row_max, jnp.minimum(row_max, 0.0)))

    def search_cond(c):
        return (c[0] < BISECT_STEPS) & (c[3] > 0)

    def search_body(c):
        it, lo, hi, _ = c
        go_on = any_open(lo, hi)
        mid = lo + (hi - lo) * 0.5
        cnt = count(ge, mid)
        hit = cnt == kf
        above = cnt > kf
        lo = jnp.where(hit | above, mid, lo)
        hi = jnp.where(hit | jnp.logical_not(above), mid, hi)
        return it + 1, lo, hi, go_on

    _, lo, hi, _ = lax.while_loop(search_cond, search_body, (jnp.int32(0), lo, hi, any_open(lo, hi)))
    still_open = any_open(lo, hi)

    def peel_cond(c):
        return (c[0] < topk + 2) & (c[3] > 0)

    def peel_body(c):
        it, lo, hi, _ = c
        is_open = hi > lo
        widen = jnp.where(it == 0, jnp.finfo(F32).max, 0.0)
        below = jnp.where(hi == row_max, hi + widen, hi)
        top = reduce_keys(lambda t, rows: jnp.where((t >= lo[rows]) & (t < below[rows]), t, -jnp.inf),
                          -jnp.inf, jnp.maximum, jnp.max)
        found = count(ge, top) >= kf
        lo = jnp.where(is_open & found, top, lo)
        hi = jnp.where(is_open, top, hi)
        return it + 1, lo, hi, any_open(lo, hi)

    _, thr, _, _ = lax.while_loop(peel_cond, peel_body, (jnp.int32(0), lo, hi, still_open))

    has_excess = jnp.max(jnp.where(count(ge, thr) > kf, 1.0, 0.0)) > 0.0
    thr_c = thr[:, 0:1]

    def write_plain(ci, carry):
        k0 = pl.multiple_of(ci * tc, tc)
        sc = s_ref[:, pl.ds(k0, tc)]
        o_ref[0, :, pl.ds(k0, tc)] = jnp.where(sc >= thr_c, 0.0, MASK_BIAS).astype(o_ref.dtype)
        return carry

    @pl.when(jnp.logical_not(has_excess))
    def _():
        lax.fori_loop(0, n_c, write_plain, 0)

    @pl.when(has_excess)
    def _():
        need_c = (kf - count(lambda a, b: a > b, thr))[:, 0:1]

        def write_ties(ci, seen):
            k0 = pl.multiple_of(ci * tc, tc)
            sc = s_ref[:, pl.ds(k0, tc)]
            eq = sc == thr_c
            r = lax.broadcasted_iota(jnp.int32, (tc, tc), 0)
            cidx = lax.broadcasted_iota(jnp.int32, (tc, tc), 1)
            upper = jnp.where(r < cidx, 1.0, 0.0).astype(BF16)
            before = seen + _dot(jnp.where(eq, 1.0, 0.0).astype(BF16), upper)
            keep = (sc > thr_c) | (eq & (before < need_c))
            o_ref[0, :, pl.ds(k0, tc)] = jnp.where(keep, 0.0, MASK_BIAS).astype(o_ref.dtype)
            return seen + jnp.sum(jnp.where(eq, 1.0, 0.0), axis=1, keepdims=True)

        lax.fori_loop(0, n_c, write_ties, jnp.zeros((tq, 1), F32))

    def write_rest(ci, carry):
        k0 = pl.multiple_of(ci * tc, tc)
        o_ref[0, :, pl.ds(k0, tc)] = jnp.full((tq, tc), MASK_BIAS, o_ref.dtype)
        return carry

    lax.fori_loop(n_c, nchunks_total, write_rest, 0)


def _dsa_select(qi, smalls, ki, *, tq, tc, topk):
    b, l, wi = qi.shape
    return pl.pallas_call(
        functools.partial(_dsa_select_body, tq=tq, tc=tc, topk=topk),
        grid=(b, l // tq),
        in_specs=[pl.BlockSpec((1, tq, wi), lambda bi, i: (bi, i, 0)),
                  pl.BlockSpec((1, tq, LANES), lambda bi, i: (bi, i, 0)),
                  pl.BlockSpec((1, l, IDX_HEAD_DIM), lambda bi, i: (bi, 0, 0))],
        out_specs=pl.BlockSpec((1, tq, l), lambda bi, i: (bi, i, 0)),
        out_shape=jax.ShapeDtypeStruct((b, l, l), BF16),
        scratch_shapes=[pltpu.VMEM((tq, l), F32)],
        compiler_params=_cparams("parallel", "parallel"),
        name="dsa_select",
    )(qi, smalls, ki)


def _dsa_attn_body(q_ref, k_ref, v_ref, b_ref, o_ref, m_ref, l_ref, acc_ref):
    qi, ki = pl.program_id(1), pl.program_id(2)
    d = DSA_HEAD_DIM

    @pl.when(ki == 0)
    def _():
        m_ref[...] = jnp.full_like(m_ref, MASK_BIAS)
        l_ref[...] = jnp.zeros_like(l_ref)
        acc_ref[...] = jnp.zeros_like(acc_ref)

    @pl.when(ki <= qi)
    def _():
        bias = b_ref[0].astype(F32)
        for h in range(DSA_HEADS):
            lo = h * d
            s = _dot_nt(q_ref[0, :, lo:lo + d], k_ref[0, :, lo:lo + d]) + bias
            m_old = m_ref[h]
            m_new = jnp.maximum(m_old, jnp.max(s, axis=-1, keepdims=True))
            alpha = jnp.exp(m_old - m_new)
            p = jnp.exp(s - m_new[:, 0:1])
            l_ref[h] = alpha * l_ref[h] + jnp.sum(p, axis=-1, keepdims=True)
            acc_ref[h] = alpha * acc_ref[h] + _dot(p.astype(BF16), v_ref[0, :, lo:lo + d])
            m_ref[h] = m_new

    @pl.when(ki == pl.num_programs(2) - 1)
    def _():
        for h in range(DSA_HEADS):
            lo = h * d
            o_ref[0, :, lo:lo + d] = (acc_ref[h] / l_ref[h]).astype(o_ref.dtype)


def _dsa_attention(qb, kb, proj, bias, *, t):
    b, l, w = qb.shape
    n = l // t
    kv_idx = lambda bi, i, j: (bi, jnp.minimum(i, j), 0)
    v_col = (COL_QKV_B + 2 * DSA_WIDTH) // DSA_WIDTH
    return pl.pallas_call(
        _dsa_attn_body,
        grid=(b, n, n),
        in_specs=[pl.BlockSpec((1, t, w), lambda bi, i, j: (bi, i, 0)),
                  pl.BlockSpec((1, t, w), kv_idx),
                  pl.BlockSpec((1, t, w), lambda bi, i, j: (bi, jnp.minimum(i, j), v_col)),
                  pl.BlockSpec((1, t, t), lambda bi, i, j: (bi, i, jnp.minimum(i, j)))],
        out_specs=pl.BlockSpec((1, t, w), lambda bi, i, j: (bi, i, 0)),
        out_shape=jax.ShapeDtypeStruct((b, l, w), BF16),
        scratch_shapes=[pltpu.VMEM((DSA_HEADS, t, LANES), F32),
                        pltpu.VMEM((DSA_HEADS, t, LANES), F32),
                        pltpu.VMEM((DSA_HEADS, t, DSA_HEAD_DIM), F32)],
        compiler_params=_cparams("parallel", "parallel", "arbitrary"),
        name="dsa_attention",
    )(qb, kb, proj, bias)


def _rope_tables(seq_len):
    def tab(dim):
        inv_freq = 1.0 / (ROPE_THETA ** (jnp.arange(0, dim, 2, dtype=F32) / dim))
        ang = jnp.arange(seq_len, dtype=F32)[:, None] * inv_freq[None, :]
        c, s = jnp.cos(ang), jnp.sin(ang)
        return jnp.concatenate([c, c], axis=1), jnp.concatenate([-s, s], axis=1)
    ca, sa = tab(DSA_HEAD_DIM)
    c64, s64 = tab(IDX_HEAD_DIM)
    ci = jnp.concatenate([c64, c64], axis=1)
    si = jnp.concatenate([s64, s64], axis=1)
    return ca, sa, ci, si


def _regroup_w_in(w):
    sizes = (3 * GDN_WIDTH, GDN_WIDTH, GDN_HEADS, GDN_HEADS, 3 * DSA_WIDTH,
             IDX_HEADS * IDX_HEAD_DIM, IDX_HEAD_DIM, IDX_HEADS, 1024, 1024)
    parts, start = [], 0
    for s in sizes:
        parts.append(w[:, start:start + s])
        start += s
    qkv_a, z_a, a_a, b_a, qkv_b, q_i, k_i, w_i, g_a, g_b = parts
    main = jnp.concatenate([qkv_a, z_a, qkv_b, q_i, g_a, g_b], axis=1).astype(BF16)
    pad = jnp.zeros((w.shape[0], SM_K - SM_W - IDX_HEADS), w.dtype)
    smalls = jnp.concatenate([a_a, b_a, w_i, pad, k_i], axis=1)
    return main, smalls


def kernel(x, norm_mix, w_in, conv_w, a_log, dt_bias, gdn_norm, w_out_gdn, w_out_dsa, w_o,
           norm_ffn, w_gate_up, w_down, norm_final):
    b, l, dm = x.shape
    depth = w_in.shape[0]
    m = b * l
    topk = min(TOPK_MAX, l // 4)
    tabs = _rope_tables(l)
    tm = min(1024, m)
    t_attn = min(512, l)
    tq_sel = min(256, l)
    h = x.reshape(m, dm)
    for layer in range(depth):
        w_main, w_smalls = _regroup_w_in(w_in[layer])
        proj = _norm_matmul(h, norm_mix[layer], w_main, tm=min(512, m), tn=MAIN_COLS // 4, out_dtype=BF16)
        smalls = _norm_matmul(h, norm_mix[layer], w_smalls, tm=tm, tn=LANES, out_dtype=F32, exact=True)
        proj = proj.reshape(b, l, MAIN_COLS)
        smalls = smalls.reshape(b, l, LANES)

        q_a, k_a, v_a = _gdn_prep(proj, conv_w[layer], tl=min(256, l))
        o_a = _gdn_chunk(q_a, k_a, v_a, proj, smalls, a_log[layer], dt_bias[layer], gdn_norm[layer],
                         tl=min(512, l))

        q_b, k_b, q_i, k_i = _dsa_prep(proj, smalls, tabs, tl=min(512, l))
        bias = _dsa_select(q_i, smalls, k_i, tq=tq_sel, tc=min(512, l), topk=topk)
        o_b = _dsa_attention(q_b, k_b, proj, bias, t=t_attn)

        merged = _gated_merge(o_a.reshape(m, GDN_WIDTH), o_b.reshape(m, DSA_WIDTH),
                              w_out_gdn[layer].astype(BF16), w_out_dsa[layer].astype(BF16),
                              proj.reshape(m, MAIN_COLS), tm=tm, tn=512)
        h = _matmul_resid(merged, w_o[layer].astype(BF16), h, tm=min(512, m))
        act = _norm_swiglu(h, norm_ffn[layer], w_gate_up[layer].astype(BF16), tm=min(512, m),
                           tn=w_gate_up.shape[2] // 4)
        last = layer == depth - 1
        h = _matmul_resid(act, w_down[layer].astype(BF16), h, tm=min(512, m),
                          final_gain=norm_final if last else None)
    return h.reshape(b, l, dm)
```

```python
import functools
import math

import jax
import jax.numpy as jnp
from jax import lax
from jax.experimental import pallas as pl
from jax.experimental.pallas import tpu as pltpu

F32 = jnp.float32
BF16 = jnp.bfloat16
HIGHEST = lax.Precision.HIGHEST

GDN_HEADS = 8
GDN_HEAD_DIM = 128
GDN_WIDTH = GDN_HEADS * GDN_HEAD_DIM
CONV_K = 4
GDN_CHUNK = 64
DSA_HEADS = 8
DSA_HEAD_DIM = 128
DSA_WIDTH = DSA_HEADS * DSA_HEAD_DIM
IDX_HEADS = 8
IDX_HEAD_DIM = 64
TOPK_MAX = 256
ROPE_THETA = 10000.0
NORM_EPS = 1e-6

LANES = 128
VMEM_LIMIT = 48 * 1024 * 1024
BISECT_STEPS = 32
SELECT_ROW_BLOCK = 64
MASK_BIAS = -(2.0 ** 100)
Q_SCALE = DSA_HEAD_DIM ** -0.5 * math.log2(math.e)

COL_QKV_A = 0
COL_Z_A = 3 * GDN_WIDTH
COL_QKV_B = COL_Z_A + GDN_WIDTH
COL_G_A = COL_QKV_B + 3 * DSA_WIDTH
COL_Q_I = COL_G_A + 2 * 1024
MAIN_COLS = COL_Q_I + IDX_HEADS * IDX_HEAD_DIM
SM_A, SM_B, SM_W, SM_K = 0, 8, 16, 64


def _cparams(*sem):
    return pltpu.CompilerParams(dimension_semantics=sem, vmem_limit_bytes=VMEM_LIMIT)


def _rmsnorm_rows(x, gain):
    ms = jnp.mean(x * x, axis=-1, keepdims=True)
    return x * lax.rsqrt(ms + NORM_EPS) * gain


def _dot(a, b):
    return jnp.dot(a, b, preferred_element_type=F32)


def _dot_nt(a, b):
    return lax.dot_general(a, b, (((1,), (1,)), ((), ())), preferred_element_type=F32)


def _resident(shape):
    return pl.BlockSpec(shape, lambda i, j: (0,) * len(shape), pipeline_mode=pl.Buffered(1))


def _norm_mm_body(x_ref, g_ref, w_ref, o_ref, xn_ref, *, tn, exact):
    @pl.when(pl.program_id(1) == 0)
    def _():
        xn_ref[...] = _rmsnorm_rows(x_ref[...], g_ref[...]).astype(xn_ref.dtype)

    w = w_ref[:, pl.ds(pl.multiple_of(pl.program_id(1) * tn, tn), tn)]
    if exact:
        o = jnp.dot(xn_ref[...], w, precision=HIGHEST, preferred_element_type=F32)
    else:
        o = _dot(xn_ref[...], w)
    o_ref[...] = o.astype(o_ref.dtype)


def _norm_matmul(x, gain, w, *, tm, tn, out_dtype, exact=False):
    m, k = x.shape
    n = w.shape[1]
    return pl.pallas_call(
        functools.partial(_norm_mm_body, tn=tn, exact=exact),
        grid=(m // tm, n // tn),
        in_specs=[pl.BlockSpec((tm, k), lambda i, j: (i, 0)),
                  _resident((1, k)),
                  _resident((k, n))],
        out_specs=pl.BlockSpec((tm, tn), lambda i, j: (i, j)),
        out_shape=jax.ShapeDtypeStruct((m, n), out_dtype),
        scratch_shapes=[pltpu.VMEM((tm, k), w.dtype)],
        compiler_params=_cparams("parallel", "arbitrary"),
        name="norm_matmul_exact" if exact else "norm_matmul",
    )(x, gain.reshape(1, k), w)


def _resident1(shape):
    return pl.BlockSpec(shape, lambda i: (0,) * len(shape), pipeline_mode=pl.Buffered(1))


def _ffn_body(x_ref, g_ref, wgu_ref, wd_ref, *rest, f, final_norm):
    if final_norm:
        gf_ref, o_ref = rest
    else:
        (o_ref,) = rest
    x = x_ref[...]
    xn = _rmsnorm_rows(x, g_ref[...]).astype(BF16)
    g = _dot(xn, wgu_ref[:, :f])
    u = _dot(xn, wgu_ref[:, f:])
    act = (g * jax.nn.sigmoid(g) * u).astype(BF16)
    h = x + _dot(act, wd_ref[...])
    if final_norm:
        h = _rmsnorm_rows(h, gf_ref[...])
    o_ref[...] = h


def _ffn(x, gain, w_gate_up, w_down, *, tm, final_gain=None):
    m, k = x.shape
    f = w_down.shape[0]
    final_norm = final_gain is not None
    in_specs = [pl.BlockSpec((tm, k), lambda i: (i, 0)),
                _resident1((1, k)), _resident1((k, 2 * f)), _resident1((f, k))]
    args = [x, gain.reshape(1, k), w_gate_up, w_down]
    if final_norm:
        in_specs.append(_resident1((1, k)))
        args.append(final_gain.reshape(1, k))
    return pl.pallas_call(
        functools.partial(_ffn_body, f=f, final_norm=final_norm),
        grid=(m // tm,),
        in_specs=in_specs,
        out_specs=pl.BlockSpec((tm, k), lambda i: (i, 0)),
        out_shape=jax.ShapeDtypeStruct((m, k), F32),
        compiler_params=_cparams("parallel"),
        name="ffn_norm" if final_norm else "ffn",
    )(*args)


def _mixer_out_body(oa_ref, ob_ref, wa_ref, wb_ref, wo_ref, ga_ref, gb_ref, h_ref, o_ref):
    ya = _dot(oa_ref[...], wa_ref[...])
    yb = _dot(ob_ref[...], wb_ref[...])
    ga = jax.nn.sigmoid(ga_ref[...].astype(F32))
    gb = jax.nn.sigmoid(gb_ref[...].astype(F32))
    merged = (ga * ya + gb * yb).astype(BF16)
    o_ref[...] = h_ref[...] + _dot(merged, wo_ref[...])


def _mixer_out(oa, ob, wa, wb, wo, proj, h, *, tm):
    m, k = oa.shape
    n = wa.shape[1]
    row = lambda w: pl.BlockSpec((tm, w), lambda i: (i, 0))
    return pl.pallas_call(
        _mixer_out_body,
        grid=(m // tm,),
        in_specs=[row(k), row(k), _resident1((k, n)), _resident1((k, n)), _resident1((n, n)),
                  pl.BlockSpec((tm, n), lambda i: (i, COL_G_A // n)),
                  pl.BlockSpec((tm, n), lambda i: (i, COL_G_A // n + 1)),
                  row(n)],
        out_specs=row(n),
        out_shape=jax.ShapeDtypeStruct((m, n), F32),
        compiler_params=_cparams("parallel"),
        name="mixer_out",
    )(oa, ob, wa, wb, wo, proj, proj, h)


HALO = 16


def _gdn_prep_body(x_ref, halo_ref, cw_ref, q_ref, k_ref, v_ref, xs_ref, *, tl):
    i = pl.program_id(1)
    halo = halo_ref[0].astype(F32)
    xs_ref[0:HALO, :] = jnp.where(i == 0, 0.0, halo)
    xs_ref[HALO:HALO + tl, :] = x_ref[0].astype(F32)
    y = cw_ref[0:1, :] * xs_ref[pl.ds(HALO - 3, tl), :]
    for j in range(1, CONV_K):
        y = y + cw_ref[j:j + 1, :] * xs_ref[pl.ds(HALO - 3 + j, tl), :]
    y = y * jax.nn.sigmoid(y)
    for h in range(GDN_HEADS):
        lo = h * GDN_HEAD_DIM
        qh = y[:, lo:lo + GDN_HEAD_DIM]
        kh = y[:, GDN_WIDTH + lo:GDN_WIDTH + lo + GDN_HEAD_DIM]
        qn = qh * lax.rsqrt(jnp.sum(qh * qh, axis=-1, keepdims=True) + NORM_EPS)
        kn = kh * lax.rsqrt(jnp.sum(kh * kh, axis=-1, keepdims=True) + NORM_EPS)
        q_ref[0, :, lo:lo + GDN_HEAD_DIM] = (qn * (GDN_HEAD_DIM ** -0.5)).astype(q_ref.dtype)
        k_ref[0, :, lo:lo + GDN_HEAD_DIM] = kn.astype(k_ref.dtype)
    v_ref[0] = y[:, 2 * GDN_WIDTH:].astype(v_ref.dtype)


def _gdn_prep(proj, conv_w, *, tl):
    b, l, _ = proj.shape
    c = 3 * GDN_WIDTH
    r = tl // HALO
    out = jax.ShapeDtypeStruct((b, l, GDN_WIDTH), BF16)
    ospec = pl.BlockSpec((1, tl, GDN_WIDTH), lambda bi, i: (bi, i, 0))
    return pl.pallas_call(
        functools.partial(_gdn_prep_body, tl=tl),
        grid=(b, l // tl),
        in_specs=[pl.BlockSpec((1, tl, c), lambda bi, i: (bi, i, 0)),
                  pl.BlockSpec((1, HALO, c), lambda bi, i: (bi, jnp.maximum(i * r - 1, 0), 0)),
                  pl.BlockSpec((CONV_K, c), lambda bi, i: (0, 0))],
        out_specs=[ospec, ospec, ospec],
        out_shape=[out, out, out],
        scratch_shapes=[pltpu.VMEM((HALO + tl, c), F32)],
        compiler_params=_cparams("parallel", "parallel"),
        name="gdn_prep",
    )(proj, proj, conv_w)


def _gdn_chunk_body(q_ref, k_ref, v_ref, z_ref, sm_ref, alog_ref, dtb_ref, gn_ref, o_ref,
                    s_ref, wq_ref, qkk_ref, u_ref, gl_ref, *, nch, nb):
    c = GDN_CHUNK
    d = GDN_HEAD_DIM
    probs = [(bb, h) for bb in range(nb) for h in range(GDN_HEADS)]

    @pl.when(pl.program_id(1) == 0)
    def _():
        s_ref[...] = jnp.zeros_like(s_ref)

    row = lax.broadcasted_iota(jnp.int32, (c, c), 0)
    col = lax.broadcasted_iota(jnp.int32, (c, c), 1)
    tril_incl = row >= col
    tril_strict = row > col
    eye = (row == col).astype(F32)
    tri_f = tril_incl.astype(F32)
    triu_f = (row <= col).astype(F32)

    def intra(ci, carry):
        r0 = pl.multiple_of(ci * c, c)
        gam_col, gam_row, beta_all = [], [], []
        for bb in range(nb):
            sm = sm_ref[bb, pl.ds(r0, c), :]
            g_all = -jnp.exp(alog_ref[...]) * jax.nn.softplus(sm + dtb_ref[...])
            beta_all.append(jax.nn.sigmoid(sm))
            gam_col.append(jnp.dot(tri_f, g_all, precision=HIGHEST, preferred_element_type=F32))
            gam_row.append(jnp.dot(g_all.T, triu_f, precision=HIGHEST, preferred_element_type=F32))
        gc = [gam_col[bb][:, SM_A + h:SM_A + h + 1] for bb, h in probs]
        bc = [beta_all[bb][:, SM_B + h:SM_B + h + 1] for bb, h in probs]
        g_last = [g[c - 1:c, :] for g in gc]
        decay = [jnp.exp(jnp.minimum(g - gam_row[bb][SM_A + h:SM_A + h + 1, :], 0.0))
                 for g, (bb, h) in zip(gc, probs)]
        qh = [q_ref[bb, pl.ds(r0, c), h * d:(h + 1) * d] for bb, h in probs]
        kh = [k_ref[bb, pl.ds(r0, c), h * d:(h + 1) * d] for bb, h in probs]
        vh = [v_ref[bb, pl.ds(r0, c), h * d:(h + 1) * d] for bb, h in probs]
        kq = [_dot_nt(jnp.concatenate([k, q], axis=0), k) for k, q in zip(kh, qh)]
        qk = [jnp.where(tril_incl, x[c:] * dc, 0.0) for x, dc in zip(kq, decay)]
        n_pow = [-jnp.where(tril_strict, b_ * x[:c] * dc, 0.0) for x, b_, dc in zip(kq, bc, decay)]
        t_inv = [eye + n for n in n_pow]
        n_pow = [_dot(n.astype(BF16), n.astype(BF16)) for n in n_pow]
        for _ in range(int(math.log2(c)) - 2):
            prod = [_dot(jnp.concatenate([n, t], axis=0).astype(BF16), n.astype(BF16))
                    for n, t in zip(n_pow, t_inv)]
            n_pow = [p[:c] for p in prod]
            t_inv = [t + p[c:] for t, p in zip(t_inv, prod)]
        t_inv = [t + _dot(t.astype(BF16), n.astype(BF16)) for n, t in zip(n_pow, t_inv)]
        e_gc = [jnp.exp(g) for g in gc]
        kf = [k.astype(F32) for k in kh]
        rhs = [jnp.concatenate([v.astype(F32) * b_, k * (b_ * e)], axis=1)
               for v, k, b_, e in zip(vh, kf, bc, e_gc)]
        uw = [_dot(t.astype(BF16), r.astype(BF16)) for t, r in zip(t_inv, rhs)]
        for i, (bb, h) in enumerate(probs):
            k_dec = kf[i] * jnp.exp(g_last[i] - gc[i])
            u_ref[bb, ci, h] = uw[i][:, :d]
            wq_ref[bb, ci, h] = jnp.concatenate([uw[i][:, d:], qh[i].astype(F32) * e_gc[i]], axis=0).astype(BF16)
            qkk_ref[bb, ci, h] = jnp.concatenate([qk[i], k_dec.T], axis=0).astype(BF16)
            gl_ref[bb, ci, h] = jnp.broadcast_to(jnp.exp(g_last[i]), (1, d))
        return carry

    lax.fori_loop(0, nch, intra, 0)

    def scan(ci, carry):
        r0 = pl.multiple_of(ci * c, c)
        s_old = [s_ref[bb, h] for bb, h in probs]
        m1 = [_dot(wq_ref[bb, ci, h], s.astype(BF16)) for s, (bb, h) in zip(s_old, probs)]
        v_new = [u_ref[bb, ci, h] - m[:c] for m, (bb, h) in zip(m1, probs)]
        m2 = [_dot(qkk_ref[bb, ci, h], v.astype(BF16)) for v, (bb, h) in zip(v_new, probs)]
        for i, (bb, h) in enumerate(probs):
            s_ref[bb, h] = s_old[i] * gl_ref[bb, ci, h] + m2[i][c:]
            o = m1[i][c:] + m2[i][:c]
            o = o * lax.rsqrt(jnp.mean(o * o, axis=-1, keepdims=True) + NORM_EPS) * gn_ref[...]
            z = z_ref[bb, pl.ds(r0, c), h * d:(h + 1) * d].astype(F32)
            o_ref[bb, pl.ds(r0, c), h * d:(h + 1) * d] = (o * (z * jax.nn.sigmoid(z))).astype(o_ref.dtype)
        return carry

    lax.fori_loop(0, nch, scan, 0)


def _gdn_chunk(q, k, v, proj, smalls, a_log, dt_bias, gdn_norm, *, tl):
    b, l, _ = q.shape
    c, d, nh = GDN_CHUNK, GDN_HEAD_DIM, GDN_HEADS
    nch = tl // c
    nb = 2 if b % 2 == 0 else 1
    pad = lambda t, at: jnp.zeros((1, LANES), F32).at[0, at:at + t.shape[0]].set(t.astype(F32))
    qspec = pl.BlockSpec((nb, tl, GDN_WIDTH), lambda bi, i: (bi, i, 0))
    vec = pl.BlockSpec((1, LANES), lambda bi, i: (0, 0))
    return pl.pallas_call(
        functools.partial(_gdn_chunk_body, nch=nch, nb=nb),
        grid=(b // nb, l // tl),
        in_specs=[qspec, qspec, qspec,
                  pl.BlockSpec((nb, tl, GDN_WIDTH), lambda bi, i: (bi, i, COL_Z_A // GDN_WIDTH)),
                  pl.BlockSpec((nb, tl, LANES), lambda bi, i: (bi, i, 0)),
                  vec, vec, vec],
        out_specs=qspec,
        out_shape=jax.ShapeDtypeStruct((b, l, GDN_WIDTH), BF16),
        scratch_shapes=[pltpu.VMEM((nb, nh, d, d), F32),
                        pltpu.VMEM((nb, nch, nh, 2 * c, d), BF16),
                        pltpu.VMEM((nb, nch, nh, c + d, c), BF16),
                        pltpu.VMEM((nb, nch, nh, c, d), F32),
                        pltpu.VMEM((nb, nch, nh, 1, d), F32)],
        compiler_params=_cparams("parallel", "arbitrary"),
        name="gdn_chunk",
    )(q, k, v, proj, smalls, pad(a_log, SM_A), pad(dt_bias, SM_A), gdn_norm.reshape(1, LANES).astype(F32))


def _swap_halves(x, half):
    if 2 * half == LANES:
        return pltpu.roll(x, half, axis=1)
    lane = lax.broadcasted_iota(jnp.int32, x.shape, 1)
    first = (lane % (2 * half)) < half
    return jnp.where(first, pltpu.roll(x, LANES - half, axis=1), pltpu.roll(x, half, axis=1))


def _dsa_prep_body(qk_ref, v_ref, qi_ref, sm_ref, ca_ref, sa_ref, ci_ref, si_ref,
                   qb_ref, kb_ref, vx_ref, qio_ref, kio_ref):
    ca, sa = ca_ref[...], sa_ref[...]
    ones = jnp.ones((v_ref.shape[1], DSA_HEAD_DIM), BF16)
    for h in range(DSA_HEADS):
        lo = h * DSA_HEAD_DIM
        vx_ref[0, :, 2 * lo:2 * lo + DSA_HEAD_DIM] = v_ref[0, :, lo:lo + DSA_HEAD_DIM]
        vx_ref[0, :, 2 * lo + DSA_HEAD_DIM:2 * lo + 2 * DSA_HEAD_DIM] = ones
    for h in range(2 * DSA_HEADS):
        lo = h * DSA_HEAD_DIM
        x = qk_ref[0, :, lo:lo + DSA_HEAD_DIM].astype(F32)
        y = x * ca + _swap_halves(x, DSA_HEAD_DIM // 2) * sa
        if h < DSA_HEADS:
            qb_ref[0, :, lo:lo + DSA_HEAD_DIM] = (y * Q_SCALE).astype(BF16)
        else:
            lo -= DSA_WIDTH
            kb_ref[0, :, lo:lo + DSA_HEAD_DIM] = y.astype(BF16)
    ci, si = ci_ref[...], si_ref[...]
    for p in range(IDX_HEADS * IDX_HEAD_DIM // LANES):
        lo = p * LANES
        x = qi_ref[0, :, lo:lo + LANES].astype(F32)
        y = x * ci + _swap_halves(x, IDX_HEAD_DIM // 2) * si
        qio_ref[0, :, lo:lo + LANES] = y.astype(BF16)
    x = sm_ref[0]
    y = x * ci + _swap_halves(x, IDX_HEAD_DIM // 2) * si
    kio_ref[0] = (y[:, SM_K:] * (IDX_HEAD_DIM ** -0.5)).astype(BF16)


def _dsa_prep(proj, smalls, tabs, *, tl):
    b, l, _ = proj.shape
    w2 = 2 * DSA_WIDTH
    wi = IDX_HEADS * IDX_HEAD_DIM
    tab = pl.BlockSpec((tl, LANES), lambda bi, i: (i, 0))
    big = lambda n: pl.BlockSpec((1, tl, n), lambda bi, i: (bi, i, 0))
    return pl.pallas_call(
        _dsa_prep_body,
        grid=(b, l // tl),
        in_specs=[pl.BlockSpec((1, tl, w2), lambda bi, i: (bi, i, COL_QKV_B // w2)),
                  pl.BlockSpec((1, tl, DSA_WIDTH), lambda bi, i: (bi, i, (COL_QKV_B + w2) // DSA_WIDTH)),
                  pl.BlockSpec((1, tl, wi), lambda bi, i: (bi, i, COL_Q_I // wi)),
                  big(LANES), tab, tab, tab, tab],
        out_specs=[big(DSA_WIDTH), big(DSA_WIDTH), big(w2), big(wi), big(IDX_HEAD_DIM)],
        out_shape=[jax.ShapeDtypeStruct((b, l, DSA_WIDTH), BF16),
                   jax.ShapeDtypeStruct((b, l, DSA_WIDTH), BF16),
                   jax.ShapeDtypeStruct((b, l, w2), BF16),
                   jax.ShapeDtypeStruct((b, l, wi), BF16),
                   jax.ShapeDtypeStruct((b, l, IDX_HEAD_DIM), BF16)],
        compiler_params=_cparams("parallel", "parallel"),
        name="dsa_prep",
    )(proj, proj, proj, smalls, *tabs)


def _dsa_select_body(qi_ref, sm_ref, ki_ref, o_ref, s_ref, *, tq, tc, topk):
    l = ki_ref.shape[1]
    nchunks_total = l // tc
    nl = tc // LANES
    q0 = pl.program_id(1) * tq
    n_c = (q0 + tq + tc - 1) // tc
    kf = float(topk)
    w_all = sm_ref[0] * (IDX_HEADS ** -0.5)
    qpos = q0 + lax.broadcasted_iota(jnp.int32, (tq, tc), 0)
    kloc = lax.broadcasted_iota(jnp.int32, (tq, tc), 1)
    rb = min(SELECT_ROW_BLOCK, tq)
    rep = lambda col: jnp.broadcast_to(col, (col.shape[0], LANES))
    tiles = lambda x: [x[:, j * LANES:(j + 1) * LANES] for j in range(nl)]

    def score_chunk(ci, carry):
        mn, mx, n_pos, n_nonneg = carry
        k0 = pl.multiple_of(ci * tc, tc)
        kc = ki_ref[0, pl.ds(k0, tc), :]
        s = jnp.zeros((tq, tc), F32)
        for h in range(IDX_HEADS):
            qh = qi_ref[0, :, h * IDX_HEAD_DIM:(h + 1) * IDX_HEAD_DIM]
            s = s + w_all[:, SM_W + h:SM_W + h + 1] * jnp.maximum(_dot_nt(qh, kc), 0.0)
        causal = k0 + kloc <= qpos
        s_lo = jnp.where(causal, s, -jnp.inf)
        s_ref[:, pl.ds(k0, tc)] = s_lo
        for lo_t, hi_t in zip(tiles(s_lo), tiles(jnp.where(causal, s, jnp.inf))):
            mx = jnp.maximum(mx, lo_t)
            mn = jnp.minimum(mn, hi_t)
            n_pos = n_pos + jnp.where(lo_t > 0.0, 1.0, 0.0)
            n_nonneg = n_nonneg + jnp.where(lo_t >= 0.0, 1.0, 0.0)
        return mn, mx, n_pos, n_nonneg

    zeros = jnp.zeros((tq, LANES), F32)
    mn, mx, n_pos, n_nonneg = lax.fori_loop(0, n_c, score_chunk,
                                            (zeros + jnp.inf, zeros - jnp.inf, zeros, zeros))
    row_min = rep(jnp.min(mn, axis=1, keepdims=True))
    row_max = rep(jnp.max(mx, axis=1, keepdims=True))
    n_pos = rep(jnp.sum(n_pos, axis=1, keepdims=True))
    n_nonneg = rep(jnp.sum(n_nonneg, axis=1, keepdims=True))

    def reduce_keys(fn, init, combine, finish):
        accs = []
        for r in range(tq // rb):
            rows = slice(r * rb, (r + 1) * rb)

            def body(ci, acc, rows=rows):
                k0 = pl.multiple_of(ci * tc, tc)
                for t in tiles(s_ref[rows, pl.ds(k0, tc)]):
                    acc = combine(acc, fn(t, rows))
                return acc
            accs.append(lax.fori_loop(0, n_c, body, jnp.full((rb, LANES), init, F32)))
        return rep(finish(jnp.concatenate(accs, axis=0), axis=1, keepdims=True))

    def count(op, th):
        return reduce_keys(lambda t, rows: jnp.where(op(t, th[rows]), 1.0, 0.0), 0.0, jnp.add, jnp.sum)

    ge = lambda a, b: a >= b
    any_open = lambda lo, hi: (jnp.max(hi - lo) > 0.0).astype(jnp.int32)

    n_causal = (q0 + 1 + lax.broadcasted_iota(jnp.int32, (tq, LANES), 0)).astype(F32)
    few = n_causal <= kf
    starts_closed = few | ((n_pos < kf) & (n_nonneg >= kf))
    closed_at = jnp.where(few, -jnp.finfo(F32).max, 0.0)
    positive = n_pos >= kf
    lo = jnp.where(starts_closed, closed_at, jnp.where(positive, jnp.maximum(row_min, 0.0), row_min))
    hi = jnp.where(starts_closed, closed_at, jnp.where(positive, row_max, jnp.minimum(row_max, 0.0)))

    def search_cond(c):
        return (c[0] < BISECT_STEPS) & (c[3] > 0)

    def search_body(c):
        it, lo, hi, _ = c
        go_on = any_open(lo, hi)
        mid = lo + (hi - lo) * 0.5
        cnt = count(ge, mid)
        hit = cnt == kf
        above = cnt > kf
        lo = jnp.where(hit | above, mid, lo)
        hi = jnp.where(hit | jnp.logical_not(above), mid, hi)
        return it + 1, lo, hi, go_on

    _, lo, hi, _ = lax.while_loop(search_cond, search_body, (jnp.int32(0), lo, hi, any_open(lo, hi)))
    still_open = any_open(lo, hi)

    def peel_cond(c):
        return (c[0] < topk + 2) & (c[3] > 0)

    def peel_body(c):
        it, lo, hi, _ = c
        is_open = hi > lo
        widen = jnp.where(it == 0, jnp.finfo(F32).max, 0.0)
        below = jnp.where(hi == row_max, hi + widen, hi)
        top = reduce_keys(lambda t, rows: jnp.where((t >= lo[rows]) & (t < below[rows]), t, -jnp.inf),
                          -jnp.inf, jnp.maximum, jnp.max)
        found = count(ge, top) >= kf
        lo = jnp.where(is_open & found, top, lo)
        hi = jnp.where(is_open, top, hi)
        return it + 1, lo, hi, any_open(lo, hi)

    _, thr, _, _ = lax.while_loop(peel_cond, peel_body, (jnp.int32(0), lo, hi, still_open))

    has_excess = jnp.max(jnp.where(count(ge, thr) > kf, 1.0, 0.0)) > 0.0
    thr_c = thr[:, 0:1]

    def write_plain(ci, carry):
        k0 = pl.multiple_of(ci * tc, tc)
        sc = s_ref[:, pl.ds(k0, tc)]
        o_ref[0, :, pl.ds(k0, tc)] = jnp.where(sc >= thr_c, 0.0, MASK_BIAS).astype(o_ref.dtype)
        return carry

    @pl.when(jnp.logical_not(has_excess))
    def _():
        lax.fori_loop(0, n_c, write_plain, 0)

    @pl.when(has_excess)
    def _():
        need_c = (kf - count(lambda a, b: a > b, thr))[:, 0:1]

        def write_ties(ci, seen):
            k0 = pl.multiple_of(ci * tc, tc)
            sc = s_ref[:, pl.ds(k0, tc)]
            eq = sc == thr_c
            r = lax.broadcasted_iota(jnp.int32, (tc, tc), 0)
            cidx = lax.broadcasted_iota(jnp.int32, (tc, tc), 1)
            upper = jnp.where(r < cidx, 1.0, 0.0).astype(BF16)
            before = seen + _dot(jnp.where(eq, 1.0, 0.0).astype(BF16), upper)
            keep = (sc > thr_c) | (eq & (before < need_c))
            o_ref[0, :, pl.ds(k0, tc)] = jnp.where(keep, 0.0, MASK_BIAS).astype(o_ref.dtype)
            return seen + jnp.sum(jnp.where(eq, 1.0, 0.0), axis=1, keepdims=True)

        lax.fori_loop(0, n_c, write_ties, jnp.zeros((tq, 1), F32))

    def write_rest(ci, carry):
        k0 = pl.multiple_of(ci * tc, tc)
        o_ref[0, :, pl.ds(k0, tc)] = jnp.full((tq, tc), MASK_BIAS, o_ref.dtype)
        return carry

    lax.fori_loop(n_c, nchunks_total, write_rest, 0)


def _dsa_select(qi, smalls, ki, *, tq, tc, topk):
    b, l, wi = qi.shape
    return pl.pallas_call(
        functools.partial(_dsa_select_body, tq=tq, tc=tc, topk=topk),
        grid=(b, l // tq),
        in_specs=[pl.BlockSpec((1, tq, wi), lambda bi, i: (bi, i, 0)),
                  pl.BlockSpec((1, tq, LANES), lambda bi, i: (bi, i, 0)),
                  pl.BlockSpec((1, l, IDX_HEAD_DIM), lambda bi, i: (bi, 0, 0))],
        out_specs=pl.BlockSpec((1, tq, l), lambda bi, i: (bi, i, 0)),
        out_shape=jax.ShapeDtypeStruct((b, l, l), BF16),
        scratch_shapes=[pltpu.VMEM((tq, l), F32)],
        compiler_params=_cparams("parallel", "parallel"),
        name="dsa_select",
    )(qi, smalls, ki)


def _dsa_attn_body(q_ref, k_ref, v_ref, b_ref, o_ref, m_ref, acc_ref):
    qi, ki = pl.program_id(1), pl.program_id(2)
    d = DSA_HEAD_DIM

    @pl.when(ki == 0)
    def _():
        m_ref[...] = jnp.full_like(m_ref, MASK_BIAS)
        acc_ref[...] = jnp.zeros_like(acc_ref)

    @pl.when(ki <= qi)
    def _():
        bias = b_ref[0]
        for h in range(DSA_HEADS):
            lo = h * d
            s = _dot_nt(q_ref[0, :, lo:lo + d], k_ref[0, :, lo:lo + d]).astype(BF16) + bias
            m_old = m_ref[h]
            m_new = jnp.maximum(m_old, jnp.max(s, axis=-1, keepdims=True).astype(F32))
            alpha = jnp.exp2(m_old - m_new)
            p = jnp.exp2(s - m_new[:, 0:1].astype(BF16))
            upd = _dot(p, v_ref[0, :, 2 * lo:2 * lo + 2 * d])
            acc_ref[h, :, :d] = alpha * acc_ref[h, :, :d] + upd[:, :d]
            acc_ref[h, :, d:] = alpha * acc_ref[h, :, d:] + upd[:, d:]
            m_ref[h] = m_new

    @pl.when(ki == pl.num_programs(2) - 1)
    def _():
        for h in range(DSA_HEADS):
            lo = h * d
            acc = acc_ref[h]
            o_ref[0, :, lo:lo + d] = (acc[:, :d] / acc[:, d:]).astype(o_ref.dtype)


def _dsa_attention(qb, kb, vx, bias, *, t):
    b, l, w = qb.shape
    n = l // t
    kv_idx = lambda bi, i, j: (bi, jnp.minimum(i, j), 0)
    return pl.pallas_call(
        _dsa_attn_body,
        grid=(b, n, n),
        in_specs=[pl.BlockSpec((1, t, w), lambda bi, i, j: (bi, i, 0)),
                  pl.BlockSpec((1, t, w), kv_idx),
                  pl.BlockSpec((1, t, 2 * w), kv_idx),
                  pl.BlockSpec((1, t, t), lambda bi, i, j: (bi, i, jnp.minimum(i, j)))],
        out_specs=pl.BlockSpec((1, t, w), lambda bi, i, j: (bi, i, 0)),
        out_shape=jax.ShapeDtypeStruct((b, l, w), BF16),
        scratch_shapes=[pltpu.VMEM((DSA_HEADS, t, LANES), F32),
                        pltpu.VMEM((DSA_HEADS, t, 2 * DSA_HEAD_DIM), F32)],
        compiler_params=_cparams("parallel", "parallel", "arbitrary"),
        name="dsa_attention",
    )(qb, kb, vx, bias)


def _rope_tables(seq_len):
    def tab(dim):
        inv_freq = 1.0 / (ROPE_THETA ** (jnp.arange(0, dim, 2, dtype=F32) / dim))
        ang = jnp.arange(seq_len, dtype=F32)[:, None] * inv_freq[None, :]
        c, s = jnp.cos(ang), jnp.sin(ang)
        return jnp.concatenate([c, c], axis=1), jnp.concatenate([-s, s], axis=1)
    ca, sa = tab(DSA_HEAD_DIM)
    c64, s64 = tab(IDX_HEAD_DIM)
    ci = jnp.concatenate([c64, c64], axis=1)
    si = jnp.concatenate([s64, s64], axis=1)
    return ca, sa, ci, si


def _regroup_w_in(w):
    sizes = (3 * GDN_WIDTH, GDN_WIDTH, GDN_HEADS, GDN_HEADS, 3 * DSA_WIDTH,
             IDX_HEADS * IDX_HEAD_DIM, IDX_HEAD_DIM, IDX_HEADS, 1024, 1024)
    parts, start = [], 0
    for s in sizes:
        parts.append(w[:, start:start + s])
        start += s
    qkv_a, z_a, a_a, b_a, qkv_b, q_i, k_i, w_i, g_a, g_b = parts
    main = jnp.concatenate([qkv_a, z_a, qkv_b, g_a, g_b, q_i], axis=1).astype(BF16)
    pad = jnp.zeros((w.shape[0], SM_K - SM_W - IDX_HEADS), w.dtype)
    smalls = jnp.concatenate([a_a, b_a, w_i, pad, k_i], axis=1)
    return main, smalls


def kernel(x, norm_mix, w_in, conv_w, a_log, dt_bias, gdn_norm, w_out_gdn, w_out_dsa, w_o,
           norm_ffn, w_gate_up, w_down, norm_final):
    b, l, dm = x.shape
    depth = w_in.shape[0]
    m = b * l
    topk = min(TOPK_MAX, l // 4)
    tabs = _rope_tables(l)
    tm = min(1024, m)
    t_attn = min(512, l)
    tq_sel = min(256, l)
    h = x.reshape(m, dm)
    for layer in range(depth):
        w_main, w_smalls = _regroup_w_in(w_in[layer])
        proj = _norm_matmul(h, norm_mix[layer], w_main, tm=min(512, m), tn=MAIN_COLS // 4, out_dtype=BF16)
        smalls = _norm_matmul(h, norm_mix[layer], w_smalls, tm=tm, tn=LANES, out_dtype=F32, exact=True)
        proj = proj.reshape(b, l, MAIN_COLS)
        smalls = smalls.reshape(b, l, LANES)

        q_a, k_a, v_a = _gdn_prep(proj, conv_w[layer], tl=min(256, l))
        o_a = _gdn_chunk(q_a, k_a, v_a, proj, smalls, a_log[layer], dt_bias[layer], gdn_norm[layer],
                         tl=min(512, l))

        q_b, k_b, v_x, q_i, k_i = _dsa_prep(proj, smalls, tabs, tl=min(512, l))
        bias = _dsa_select(q_i, smalls, k_i, tq=tq_sel, tc=min(512, l), topk=topk)
        o_b = _dsa_attention(q_b, k_b, v_x, bias, t=t_attn)

        h = _mixer_out(o_a.reshape(m, GDN_WIDTH), o_b.reshape(m, DSA_WIDTH),
                       w_out_gdn[layer].astype(BF16), w_out_dsa[layer].astype(BF16), w_o[layer].astype(BF16),
                       proj.reshape(m, MAIN_COLS), h, tm=min(512, m))
        last = layer == depth - 1
        h = _ffn(h, norm_ffn[layer], w_gate_up[layer].astype(BF16), w_down[layer].astype(BF16),
                 tm=min(512, m), final_gain=norm_final if last else None)
    return h.reshape(b, l, dm)
```

```python
import functools
import math

import jax
import jax.numpy as jnp
from jax import lax
from jax.experimental import pallas as pl
from jax.experimental.pallas import tpu as pltpu

F32 = jnp.float32
BF16 = jnp.bfloat16
HIGHEST = lax.Precision.HIGHEST

GDN_HEADS = 8
GDN_HEAD_DIM = 128
GDN_WIDTH = GDN_HEADS * GDN_HEAD_DIM
CONV_K = 4
GDN_CHUNK = 64
DSA_HEADS = 8
DSA_HEAD_DIM = 128
DSA_WIDTH = DSA_HEADS * DSA_HEAD_DIM
IDX_HEADS = 8
IDX_HEAD_DIM = 64
TOPK_MAX = 256
ROPE_THETA = 10000.0
NORM_EPS = 1e-6

LANES = 128
VMEM_LIMIT = 48 * 1024 * 1024
BISECT_STEPS = 32
SELECT_ROW_BLOCK = 64
ATTN_HEAD_GROUP = 4
MASK_BIAS = -(2.0 ** 100)
Q_SCALE = DSA_HEAD_DIM ** -0.5 * math.log2(math.e)

COL_QKV_A = 0
COL_Z_A = 3 * GDN_WIDTH
COL_QKV_B = COL_Z_A + GDN_WIDTH
COL_G_A = COL_QKV_B + 3 * DSA_WIDTH
COL_Q_I = COL_G_A + 2 * 1024
MAIN_COLS = COL_Q_I + IDX_HEADS * IDX_HEAD_DIM
SM_A, SM_B, SM_W, SM_K = 0, 8, 16, 64


def _cparams(*sem):
    return pltpu.CompilerParams(dimension_semantics=sem, vmem_limit_bytes=VMEM_LIMIT)


def _rmsnorm_rows(x, gain):
    ms = jnp.mean(x * x, axis=-1, keepdims=True)
    return x * lax.rsqrt(ms + NORM_EPS) * gain


def _dot(a, b):
    return jnp.dot(a, b, preferred_element_type=F32)


def _dot_nt(a, b):
    return lax.dot_general(a, b, (((1,), (1,)), ((), ())), preferred_element_type=F32)


def _resident(shape):
    return pl.BlockSpec(shape, lambda i, j: (0,) * len(shape), pipeline_mode=pl.Buffered(1))


def _norm_mm_body(x_ref, g_ref, w_ref, o_ref, xn_ref, *, tn, exact):
    @pl.when(pl.program_id(1) == 0)
    def _():
        xn_ref[...] = _rmsnorm_rows(x_ref[...], g_ref[...]).astype(xn_ref.dtype)

    w = w_ref[:, pl.ds(pl.multiple_of(pl.program_id(1) * tn, tn), tn)]
    if exact:
        x = xn_ref[...]
        x_hi, w_hi = x.astype(BF16), w.astype(BF16)
        x_lo = (x - x_hi.astype(F32)).astype(BF16)
        w_lo = (w - w_hi.astype(F32)).astype(BF16)
        o = _dot(x_hi, w_hi) + (_dot(x_hi, w_lo) + _dot(x_lo, w_hi))
    else:
        o = _dot(xn_ref[...], w)
    o_ref[...] = o.astype(o_ref.dtype)


def _norm_matmul(x, gain, w, *, tm, tn, out_dtype, exact=False):
    m, k = x.shape
    n = w.shape[1]
    return pl.pallas_call(
        functools.partial(_norm_mm_body, tn=tn, exact=exact),
        grid=(m // tm, n // tn),
        in_specs=[pl.BlockSpec((tm, k), lambda i, j: (i, 0)),
                  _resident((1, k)),
                  _resident((k, n))],
        out_specs=pl.BlockSpec((tm, tn), lambda i, j: (i, j)),
        out_shape=jax.ShapeDtypeStruct((m, n), out_dtype),
        scratch_shapes=[pltpu.VMEM((tm, k), w.dtype)],
        compiler_params=_cparams("parallel", "arbitrary"),
        name="norm_matmul_exact" if exact else "norm_matmul",
    )(x, gain.reshape(1, k), w)


def _resident1(shape):
    return pl.BlockSpec(shape, lambda i: (0,) * len(shape), pipeline_mode=pl.Buffered(1))


def _ffn_body(x_ref, g_ref, wgu_ref, wd_ref, *rest, f, final_norm):
    if final_norm:
        gf_ref, o_ref = rest
    else:
        (o_ref,) = rest
    x = x_ref[...]
    xn = _rmsnorm_rows(x, g_ref[...]).astype(BF16)
    g = _dot(xn, wgu_ref[:, :f])
    u = _dot(xn, wgu_ref[:, f:])
    act = (g * jax.nn.sigmoid(g) * u).astype(BF16)
    h = x + _dot(act, wd_ref[...])
    if final_norm:
        h = _rmsnorm_rows(h, gf_ref[...])
    o_ref[...] = h


def _ffn(x, gain, w_gate_up, w_down, *, tm, final_gain=None):
    m, k = x.shape
    f = w_down.shape[0]
    final_norm = final_gain is not None
    in_specs = [pl.BlockSpec((tm, k), lambda i: (i, 0)),
                _resident1((1, k)), _resident1((k, 2 * f)), _resident1((f, k))]
    args = [x, gain.reshape(1, k), w_gate_up, w_down]
    if final_norm:
        in_specs.append(_resident1((1, k)))
        args.append(final_gain.reshape(1, k))
    return pl.pallas_call(
        functools.partial(_ffn_body, f=f, final_norm=final_norm),
        grid=(m // tm,),
        in_specs=in_specs,
        out_specs=pl.BlockSpec((tm, k), lambda i: (i, 0)),
        out_shape=jax.ShapeDtypeStruct((m, k), F32),
        compiler_params=_cparams("parallel"),
        name="ffn_norm" if final_norm else "ffn",
    )(*args)


def _mixer_out_body(oa_ref, ob_ref, wa_ref, wb_ref, wo_ref, ga_ref, gb_ref, h_ref, o_ref):
    ya = _dot(oa_ref[...], wa_ref[...])
    yb = _dot(ob_ref[...], wb_ref[...])
    ga = jax.nn.sigmoid(ga_ref[...].astype(F32))
    gb = jax.nn.sigmoid(gb_ref[...].astype(F32))
    merged = (ga * ya + gb * yb).astype(BF16)
    o_ref[...] = h_ref[...] + _dot(merged, wo_ref[...])


def _mixer_out(oa, ob, wa, wb, wo, proj, h, *, tm):
    m, k = oa.shape
    n = wa.shape[1]
    row = lambda w: pl.BlockSpec((tm, w), lambda i: (i, 0))
    return pl.pallas_call(
        _mixer_out_body,
        grid=(m // tm,),
        in_specs=[row(k), row(k), _resident1((k, n)), _resident1((k, n)), _resident1((n, n)),
                  pl.BlockSpec((tm, n), lambda i: (i, COL_G_A // n)),
                  pl.BlockSpec((tm, n), lambda i: (i, COL_G_A // n + 1)),
                  row(n)],
        out_specs=row(n),
        out_shape=jax.ShapeDtypeStruct((m, n), F32),
        compiler_params=_cparams("parallel"),
        name="mixer_out",
    )(oa, ob, wa, wb, wo, proj, proj, h)


HALO = 16
CONV_ROWS = 128


def _gdn_prep_body(x_ref, halo_ref, cw_ref, q_ref, k_ref, v_ref, xs_ref, *, tl):
    i = pl.program_id(1)
    sub = CONV_ROWS
    xs_ref[0:HALO, :] = jnp.where(i == 0, jnp.zeros_like(halo_ref[0]), halo_ref[0])
    xs_ref[HALO:HALO + tl, :] = x_ref[0]
    t = lax.broadcasted_iota(jnp.int32, ((CONV_K - 1) * sub, HALO + sub), 0)
    c = lax.broadcasted_iota(jnp.int32, ((CONV_K - 1) * sub, HALO + sub), 1)
    shift = jnp.where(c == t % sub + HALO - (CONV_K - 1) + t // sub, 1.0, 0.0).astype(BF16)
    for r0 in range(0, tl, sub):
        win = xs_ref[r0:r0 + HALO + sub, :]
        z = _dot(shift, win)
        y = cw_ref[CONV_K - 1:CONV_K, :] * win[HALO:, :].astype(F32)
        for j in range(CONV_K - 1):
            y = y + cw_ref[j:j + 1, :] * z[j * sub:(j + 1) * sub]
        y = y * jax.nn.sigmoid(y)
        rows = slice(r0, r0 + sub)
        for h in range(GDN_HEADS):
            lo = h * GDN_HEAD_DIM
            qh = y[:, lo:lo + GDN_HEAD_DIM]
            kh = y[:, GDN_WIDTH + lo:GDN_WIDTH + lo + GDN_HEAD_DIM]
            qn = qh * lax.rsqrt(jnp.sum(qh * qh, axis=-1, keepdims=True) + NORM_EPS)
            kn = kh * lax.rsqrt(jnp.sum(kh * kh, axis=-1, keepdims=True) + NORM_EPS)
            q_ref[0, rows, lo:lo + GDN_HEAD_DIM] = (qn * (GDN_HEAD_DIM ** -0.5)).astype(q_ref.dtype)
            k_ref[0, rows, lo:lo + GDN_HEAD_DIM] = kn.astype(k_ref.dtype)
        v_ref[0, rows, :] = y[:, 2 * GDN_WIDTH:].astype(v_ref.dtype)


def _gdn_prep(proj, conv_w, *, tl):
    b, l, _ = proj.shape
    c = 3 * GDN_WIDTH
    r = tl // HALO
    out = jax.ShapeDtypeStruct((b, l, GDN_WIDTH), BF16)
    ospec = pl.BlockSpec((1, tl, GDN_WIDTH), lambda bi, i: (bi, i, 0))
    return pl.pallas_call(
        functools.partial(_gdn_prep_body, tl=tl),
        grid=(b, l // tl),
        in_specs=[pl.BlockSpec((1, tl, c), lambda bi, i: (bi, i, 0)),
                  pl.BlockSpec((1, HALO, c), lambda bi, i: (bi, jnp.maximum(i * r - 1, 0), 0)),
                  pl.BlockSpec((CONV_K, c), lambda bi, i: (0, 0))],
        out_specs=[ospec, ospec, ospec],
        out_shape=[out, out, out],
        scratch_shapes=[pltpu.VMEM((HALO + tl, c), BF16)],
        compiler_params=_cparams("parallel", "parallel"),
        name="gdn_prep",
    )(proj, proj, conv_w)


def _gdn_chunk_body(q_ref, k_ref, v_ref, z_ref, sm_ref, alog_ref, dtb_ref, gn_ref, o_ref,
                    s_ref, wq_ref, qkk_ref, u_ref, gl_ref, *, nch, nb):
    c = GDN_CHUNK
    d = GDN_HEAD_DIM
    probs = [(bb, h) for bb in range(nb) for h in range(GDN_HEADS)]

    @pl.when(pl.program_id(1) == 0)
    def _():
        s_ref[...] = jnp.zeros_like(s_ref)

    row = lax.broadcasted_iota(jnp.int32, (c, c), 0)
    col = lax.broadcasted_iota(jnp.int32, (c, c), 1)
    tril_incl = row >= col
    tril_strict = row > col
    eye = (row == col).astype(F32)
    tri_f = tril_incl.astype(F32)
    triu_f = (row <= col).astype(F32)

    def intra(ci, carry):
        r0 = pl.multiple_of(ci * c, c)
        gam_col, gam_row, beta_all = [], [], []
        for bb in range(nb):
            sm = sm_ref[bb, pl.ds(r0, c), :]
            g_all = -jnp.exp(alog_ref[...]) * jax.nn.softplus(sm + dtb_ref[...])
            beta_all.append(jax.nn.sigmoid(sm))
            gam_col.append(jnp.dot(tri_f, g_all, precision=HIGHEST, preferred_element_type=F32))
            gam_row.append(jnp.dot(g_all.T, triu_f, precision=HIGHEST, preferred_element_type=F32))
        gc = [gam_col[bb][:, SM_A + h:SM_A + h + 1] for bb, h in probs]
        bc = [beta_all[bb][:, SM_B + h:SM_B + h + 1] for bb, h in probs]
        g_last = [g[c - 1:c, :] for g in gc]
        decay = [jnp.exp(jnp.minimum(g - gam_row[bb][SM_A + h:SM_A + h + 1, :], 0.0))
                 for g, (bb, h) in zip(gc, probs)]
        qh = [q_ref[bb, pl.ds(r0, c), h * d:(h + 1) * d] for bb, h in probs]
        kh = [k_ref[bb, pl.ds(r0, c), h * d:(h + 1) * d] for bb, h in probs]
        vh = [v_ref[bb, pl.ds(r0, c), h * d:(h + 1) * d] for bb, h in probs]
        kq = [_dot_nt(jnp.concatenate([k, q], axis=0), k) for k, q in zip(kh, qh)]
        qk = [jnp.where(tril_incl, x[c:] * dc, 0.0) for x, dc in zip(kq, decay)]
        n_pow = [-jnp.where(tril_strict, b_ * x[:c] * dc, 0.0) for x, b_, dc in zip(kq, bc, decay)]
        t_inv = [eye + n for n in n_pow]
        n_pow = [_dot(n.astype(BF16), n.astype(BF16)) for n in n_pow]
        for _ in range(int(math.log2(c)) - 2):
            prod = [_dot(jnp.concatenate([n, t], axis=0).astype(BF16), n.astype(BF16))
                    for n, t in zip(n_pow, t_inv)]
            n_pow = [p[:c] for p in prod]
            t_inv = [t + p[c:] for t, p in zip(t_inv, prod)]
        t_inv = [t + _dot(t.astype(BF16), n.astype(BF16)) for n, t in zip(n_pow, t_inv)]
        e_gc = [jnp.exp(g) for g in gc]
        kf = [k.astype(F32) for k in kh]
        rhs = [jnp.concatenate([v.astype(F32) * b_, k * (b_ * e)], axis=1)
               for v, k, b_, e in zip(vh, kf, bc, e_gc)]
        uw = [_dot(t.astype(BF16), r.astype(BF16)) for t, r in zip(t_inv, rhs)]
        for i, (bb, h) in enumerate(probs):
            k_dec = kf[i] * jnp.exp(g_last[i] - gc[i])
            u_ref[bb, ci, h] = uw[i][:, :d]
            wq_ref[bb, ci, h] = jnp.concatenate([uw[i][:, d:], qh[i].astype(F32) * e_gc[i]], axis=0).astype(BF16)
            qkk_ref[bb, ci, h] = jnp.concatenate([qk[i], k_dec.T], axis=0).astype(BF16)
            gl_ref[bb, ci, h] = jnp.broadcast_to(jnp.exp(g_last[i]), (1, d))
        return carry

    lax.fori_loop(0, nch, intra, 0)

    def scan(ci, carry):
        r0 = pl.multiple_of(ci * c, c)
        s_old = [s_ref[bb, h] for bb, h in probs]
        m1 = [_dot(wq_ref[bb, ci, h], s.astype(BF16)) for s, (bb, h) in zip(s_old, probs)]
        v_new = [u_ref[bb, ci, h] - m[:c] for m, (bb, h) in zip(m1, probs)]
        m2 = [_dot(qkk_ref[bb, ci, h], v.astype(BF16)) for v, (bb, h) in zip(v_new, probs)]
        for i, (bb, h) in enumerate(probs):
            s_ref[bb, h] = s_old[i] * gl_ref[bb, ci, h] + m2[i][c:]
            o = m1[i][c:] + m2[i][:c]
            o = o * lax.rsqrt(jnp.mean(o * o, axis=-1, keepdims=True) + NORM_EPS) * gn_ref[...]
            z = z_ref[bb, pl.ds(r0, c), h * d:(h + 1) * d].astype(F32)
            o_ref[bb, pl.ds(r0, c), h * d:(h + 1) * d] = (o * (z * jax.nn.sigmoid(z))).astype(o_ref.dtype)
        return carry

    lax.fori_loop(0, nch, scan, 0)


def _gdn_chunk(q, k, v, proj, smalls, a_log, dt_bias, gdn_norm, *, tl):
    b, l, _ = q.shape
    c, d, nh = GDN_CHUNK, GDN_HEAD_DIM, GDN_HEADS
    nch = tl // c
    nb = 2 if b % 2 == 0 else 1
    pad = lambda t, at: jnp.zeros((1, LANES), F32).at[0, at:at + t.shape[0]].set(t.astype(F32))
    qspec = pl.BlockSpec((nb, tl, GDN_WIDTH), lambda bi, i: (bi, i, 0))
    vec = pl.BlockSpec((1, LANES), lambda bi, i: (0, 0))
    return pl.pallas_call(
        functools.partial(_gdn_chunk_body, nch=nch, nb=nb),
        grid=(b // nb, l // tl),
        in_specs=[qspec, qspec, qspec,
                  pl.BlockSpec((nb, tl, GDN_WIDTH), lambda bi, i: (bi, i, COL_Z_A // GDN_WIDTH)),
                  pl.BlockSpec((nb, tl, LANES), lambda bi, i: (bi, i, 0)),
                  vec, vec, vec],
        out_specs=qspec,
        out_shape=jax.ShapeDtypeStruct((b, l, GDN_WIDTH), BF16),
        scratch_shapes=[pltpu.VMEM((nb, nh, d, d), F32),
                        pltpu.VMEM((nb, nch, nh, 2 * c, d), BF16),
                        pltpu.VMEM((nb, nch, nh, c + d, c), BF16),
                        pltpu.VMEM((nb, nch, nh, c, d), F32),
                        pltpu.VMEM((nb, nch, nh, 1, d), F32)],
        compiler_params=_cparams("parallel", "arbitrary"),
        name="gdn_chunk",
    )(q, k, v, proj, smalls, pad(a_log, SM_A), pad(dt_bias, SM_A), gdn_norm.reshape(1, LANES).astype(F32))


def _swap_halves(x, half):
    if 2 * half == LANES:
        return pltpu.roll(x, half, axis=1)
    lane = lax.broadcasted_iota(jnp.int32, x.shape, 1)
    first = (lane % (2 * half)) < half
    return jnp.where(first, pltpu.roll(x, LANES - half, axis=1), pltpu.roll(x, half, axis=1))


def _dsa_prep_body(qk_ref, v_ref, qi_ref, sm_ref, ca_ref, sa_ref, ci_ref, si_ref,
                   qb_ref, kb_ref, vx_ref, qio_ref, kio_ref):
    ca, sa = ca_ref[...], sa_ref[...]
    ones = jnp.ones((v_ref.shape[1], DSA_HEAD_DIM), BF16)
    for h in range(DSA_HEADS):
        lo = h * DSA_HEAD_DIM
        vx_ref[0, :, 2 * lo:2 * lo + DSA_HEAD_DIM] = v_ref[0, :, lo:lo + DSA_HEAD_DIM]
        vx_ref[0, :, 2 * lo + DSA_HEAD_DIM:2 * lo + 2 * DSA_HEAD_DIM] = ones
    for h in range(2 * DSA_HEADS):
        lo = h * DSA_HEAD_DIM
        x = qk_ref[0, :, lo:lo + DSA_HEAD_DIM].astype(F32)
        y = x * ca + _swap_halves(x, DSA_HEAD_DIM // 2) * sa
        if h < DSA_HEADS:
            qb_ref[0, :, lo:lo + DSA_HEAD_DIM] = (y * Q_SCALE).astype(BF16)
        else:
            lo -= DSA_WIDTH
            kb_ref[0, :, lo:lo + DSA_HEAD_DIM] = y.astype(BF16)
    ci, si = ci_ref[...], si_ref[...]
    for p in range(IDX_HEADS * IDX_HEAD_DIM // LANES):
        lo = p * LANES
        x = qi_ref[0, :, lo:lo + LANES].astype(F32)
        y = x * ci + _swap_halves(x, IDX_HEAD_DIM // 2) * si
        qio_ref[0, :, lo:lo + LANES] = y.astype(BF16)
    x = sm_ref[0]
    y = x * ci + _swap_halves(x, IDX_HEAD_DIM // 2) * si
    kio_ref[0] = (y[:, SM_K:] * (IDX_HEAD_DIM ** -0.5)).astype(BF16)


def _dsa_prep(proj, smalls, tabs, *, tl):
    b, l, _ = proj.shape
    w2 = 2 * DSA_WIDTH
    wi = IDX_HEADS * IDX_HEAD_DIM
    tab = pl.BlockSpec((tl, LANES), lambda bi, i: (i, 0))
    big = lambda n: pl.BlockSpec((1, tl, n), lambda bi, i: (bi, i, 0))
    return pl.pallas_call(
        _dsa_prep_body,
        grid=(b, l // tl),
        in_specs=[pl.BlockSpec((1, tl, w2), lambda bi, i: (bi, i, COL_QKV_B // w2)),
                  pl.BlockSpec((1, tl, DSA_WIDTH), lambda bi, i: (bi, i, (COL_QKV_B + w2) // DSA_WIDTH)),
                  pl.BlockSpec((1, tl, wi), lambda bi, i: (bi, i, COL_Q_I // wi)),
                  big(LANES), tab, tab, tab, tab],
        out_specs=[big(DSA_WIDTH), big(DSA_WIDTH), big(w2), big(wi), big(IDX_HEAD_DIM)],
        out_shape=[jax.ShapeDtypeStruct((b, l, DSA_WIDTH), BF16),
                   jax.ShapeDtypeStruct((b, l, DSA_WIDTH), BF16),
                   jax.ShapeDtypeStruct((b, l, w2), BF16),
                   jax.ShapeDtypeStruct((b, l, wi), BF16),
                   jax.ShapeDtypeStruct((b, l, IDX_HEAD_DIM), BF16)],
        compiler_params=_cparams("parallel", "parallel"),
        name="dsa_prep",
    )(proj, proj, proj, smalls, *tabs)


def _dsa_select_body(qi_ref, sm_ref, ki_ref, *rest, tq, tc, topk, n_c, tile0):
    o_ref, s_ref = rest[-2:]
    nl = tc // LANES
    half = tq // 2
    q0 = (tile0 + pl.program_id(1)) * tq
    kf = float(topk)
    w_all = sm_ref[0] * (IDX_HEADS ** -0.5)
    rb = min(SELECT_ROW_BLOCK, half)
    rep = lambda col: jnp.broadcast_to(col, (col.shape[0], LANES))
    lane_tiles = lambda x: [x[:, j * LANES:(j + 1) * LANES] for j in range(nl)]

    def score_chunk(k0, carry, last):
        mn, mx, n_pos, n_nonneg = carry
        kc = ki_ref[0, pl.ds(k0, tc), :]
        s = jnp.zeros((tq, tc), F32)
        for h in range(IDX_HEADS):
            qh = qi_ref[0, :, h * IDX_HEAD_DIM:(h + 1) * IDX_HEAD_DIM]
            s = s + w_all[:, SM_W + h:SM_W + h + 1] * jnp.maximum(_dot_nt(qh, kc), 0.0)
        if last:
            qpos = q0 + lax.broadcasted_iota(jnp.int32, (tq, tc), 0)
            causal = k0 + lax.broadcasted_iota(jnp.int32, (tq, tc), 1) <= qpos
            s_lo, s_hi = jnp.where(causal, s, -jnp.inf), jnp.where(causal, s, jnp.inf)
        else:
            s_lo = s_hi = s
        s_ref[:, pl.ds(k0, tc)] = s_lo
        for lo_t, hi_t in zip(lane_tiles(s_lo), lane_tiles(s_hi)):
            mx = jnp.maximum(mx, lo_t)
            mn = jnp.minimum(mn, hi_t)
            n_pos = n_pos + jnp.where(lo_t > 0.0, 1.0, 0.0)
            n_nonneg = n_nonneg + jnp.where(lo_t >= 0.0, 1.0, 0.0)
        return mn, mx, n_pos, n_nonneg

    zeros = jnp.zeros((tq, LANES), F32)
    stats = lax.fori_loop(0, n_c - 1, lambda ci, c: score_chunk(pl.multiple_of(ci * tc, tc), c, False),
                          (zeros + jnp.inf, zeros - jnp.inf, zeros, zeros), unroll=2)
    mn, mx, n_pos, n_nonneg = score_chunk((n_c - 1) * tc, stats, True)
    row_min = rep(jnp.min(mn, axis=1, keepdims=True))
    row_max = rep(jnp.max(mx, axis=1, keepdims=True))
    n_pos = rep(jnp.sum(n_pos, axis=1, keepdims=True))
    n_nonneg = rep(jnp.sum(n_nonneg, axis=1, keepdims=True))

    def reduce_keys(fn, init, combine, r_lo, r_hi, unrolled):
        accs = []
        for r0 in range(r_lo, r_hi, rb):
            def chunk(k0, acc, r0=r0):
                for j in range(nl):
                    acc = combine(acc, fn(s_ref[r0:r0 + rb, pl.ds(k0 + j * LANES, LANES)], r0 - r_lo))
                return acc
            acc = jnp.full((rb, LANES), init, F32)
            if unrolled:
                for ci in range(n_c):
                    acc = chunk(ci * tc, acc)
            else:
                acc = lax.fori_loop(0, n_c, lambda ci, a: chunk(pl.multiple_of(ci * tc, tc), a), acc)
            accs.append(acc)
        return jnp.concatenate(accs, axis=0)

    def count_acc(op, th, r_lo, r_hi, unrolled=True):
        return reduce_keys(lambda t, r: jnp.where(op(t, th[r:r + rb]), 1.0, 0.0), 0.0, jnp.add, r_lo, r_hi, unrolled)

    lane_max = lambda x: rep(jnp.max(x, axis=1, keepdims=True))
    lane_sum = lambda x: rep(jnp.sum(x, axis=1, keepdims=True))
    ge = lambda a, b: a >= b
    count = lambda op, th: lane_sum(count_acc(op, th, 0, tq, unrolled=False))
    is_open = lambda lo, hi: jnp.max(hi - lo) > 0.0
    middle = lambda lo, hi: lo + (hi - lo) * 0.5

    def narrow(lo, hi, mid, cnt):
        hit = cnt == kf
        above = cnt > kf
        return jnp.where(hit | above, mid, lo), jnp.where(hit | jnp.logical_not(above), mid, hi)

    n_causal = (q0 + 1 + lax.broadcasted_iota(jnp.int32, (tq, LANES), 0)).astype(F32)
    few = n_causal <= kf
    zero_tied = (n_pos < kf) & (n_nonneg >= kf)
    starts_closed = few | zero_tied
    closed_at = jnp.where(few, -jnp.finfo(F32).max, 0.0)
    positive = n_pos >= kf
    lo = jnp.where(starts_closed, closed_at, jnp.where(positive, jnp.maximum(row_min, 0.0), row_min))
    hi = jnp.where(starts_closed, closed_at, jnp.where(positive, row_max, jnp.minimum(row_max, 0.0)))
    surplus = jnp.where((zero_tied & (n_nonneg > kf)) | (jnp.logical_not(starts_closed) & (lo >= hi)), 1.0, 0.0)

    def search_cond(c):
        return (c[0] < BISECT_STEPS) & (c[7] > 0)

    def search_body(c):
        it, lo_a, hi_a, lo_b, hi_b, acc_b, mid_b, _ = c
        go_on = (is_open(lo_a, hi_a) | is_open(lo_b, hi_b)).astype(jnp.int32)
        mid_a = middle(lo_a, hi_a)
        acc_a = count_acc(ge, mid_a, 0, half)
        lo_b, hi_b = narrow(lo_b, hi_b, mid_b, lane_sum(acc_b))
        mid_b = middle(lo_b, hi_b)
        acc_b = count_acc(ge, mid_b, half, tq)
        lo_a, hi_a = narrow(lo_a, hi_a, mid_a, lane_sum(acc_a))
        return it + 1, lo_a, hi_a, lo_b, hi_b, acc_b, mid_b, go_on

    lo_a, hi_a, lo_b, hi_b = lo[:half], hi[:half], lo[half:], hi[half:]
    mid_b = middle(lo_b, hi_b)
    _, lo_a, hi_a, lo_b, hi_b, acc_b, mid_b, _ = lax.while_loop(
        search_cond, search_body,
        (jnp.int32(0), lo_a, hi_a, lo_b, hi_b, count_acc(ge, mid_b, half, tq), mid_b, jnp.int32(1)))
    lo_b, hi_b = narrow(lo_b, hi_b, mid_b, lane_sum(acc_b))
    lo = jnp.concatenate([lo_a, lo_b], axis=0)
    hi = jnp.concatenate([hi_a, hi_b], axis=0)

    def peel_cond(c):
        return (c[0] < topk + 2) & (c[4] > 0)

    def peel_body(c):
        it, lo, hi, surplus, _ = c
        still = hi > lo
        widen = jnp.where(it == 0, jnp.finfo(F32).max, 0.0)
        below = jnp.where(hi == row_max, hi + widen, hi)
        top = lane_max(reduce_keys(
            lambda t, r: jnp.where((t >= lo[r:r + rb]) & (t < below[r:r + rb]), t, -jnp.inf),
            -jnp.inf, jnp.maximum, 0, tq, False))
        n_top = count(ge, top)
        found = still & (n_top >= kf)
        lo = jnp.where(found, top, lo)
        hi = jnp.where(still, top, hi)
        surplus = jnp.where(found & (n_top > kf), 1.0, surplus)
        return it + 1, lo, hi, surplus, is_open(lo, hi).astype(jnp.int32)

    _, thr, _, surplus, _ = lax.while_loop(
        peel_cond, peel_body, (jnp.int32(0), lo, hi, surplus, is_open(lo, hi).astype(jnp.int32)))

    for r0 in range(0, tq, rb):
        rows = slice(r0, r0 + rb)
        thr_r = thr[rows]
        has_excess = jnp.max(surplus[rows]) > 0.0

        def write_plain(ci, carry, rows=rows, thr_r=thr_r):
            k0 = pl.multiple_of(ci * tc, tc)
            for j in range(nl):
                cols = pl.ds(k0 + j * LANES, LANES)
                o_ref[0, rows, cols] = jnp.where(s_ref[rows, cols] >= thr_r, 0.0, MASK_BIAS).astype(o_ref.dtype)
            return carry

        @pl.when(jnp.logical_not(has_excess))
        def _(write_plain=write_plain):
            lax.fori_loop(0, n_c, write_plain, 0)

        @pl.when(has_excess)
        def _(r0=r0, rows=rows, thr_r=thr_r):
            n_gt = lane_sum(count_acc(lambda a, b: a > b, thr_r, r0, r0 + rb, unrolled=False))
            need_c = (kf - n_gt)[:, 0:1]
            thr_c = thr_r[:, 0:1]
            r = lax.broadcasted_iota(jnp.int32, (tc, tc), 0)
            cidx = lax.broadcasted_iota(jnp.int32, (tc, tc), 1)
            upper = jnp.where(r < cidx, 1.0, 0.0).astype(BF16)

            def write_ties(ci, seen):
                k0 = pl.multiple_of(ci * tc, tc)
                sc = s_ref[rows, pl.ds(k0, tc)]
                eq = sc == thr_c
                before = seen + _dot(jnp.where(eq, 1.0, 0.0).astype(BF16), upper)
                keep = (sc > thr_c) | (eq & (before < need_c))
                o_ref[0, rows, pl.ds(k0, tc)] = jnp.where(keep, 0.0, MASK_BIAS).astype(o_ref.dtype)
                return seen + jnp.sum(jnp.where(eq, 1.0, 0.0), axis=1, keepdims=True)

            lax.fori_loop(0, n_c, write_ties, jnp.zeros((rb, 1), F32))


def _dsa_select(qi, smalls, ki, *, tq, tc, topk):
    b, l, wi = qi.shape
    per_call = tc // tq
    bias = None
    for n_c in range(1, l // tc + 1):
        tile0 = (n_c - 1) * per_call
        in_specs = [pl.BlockSpec((1, tq, wi), lambda bi, t, tile0=tile0: (bi, tile0 + t, 0)),
                    pl.BlockSpec((1, tq, LANES), lambda bi, t, tile0=tile0: (bi, tile0 + t, 0)),
                    pl.BlockSpec((1, n_c * tc, IDX_HEAD_DIM), lambda bi, t: (bi, 0, 0))]
        args = [qi, smalls, ki]
        if bias is not None:
            in_specs.append(pl.BlockSpec(memory_space=pl.ANY))
            args.append(bias)
        bias = pl.pallas_call(
            functools.partial(_dsa_select_body, tq=tq, tc=tc, topk=topk, n_c=n_c, tile0=tile0),
            grid=(b, per_call),
            in_specs=in_specs,
            out_specs=pl.BlockSpec((1, tq, n_c * tc), lambda bi, t, tile0=tile0: (bi, tile0 + t, 0)),
            out_shape=jax.ShapeDtypeStruct((b, l, l), BF16),
            scratch_shapes=[pltpu.VMEM((tq, n_c * tc), F32)],
            input_output_aliases={} if len(args) == 3 else {3: 0},
            compiler_params=_cparams("parallel", "parallel"),
            name=f"dsa_select_{n_c}",
        )(*args)
    return bias


def _dsa_attn_body(qt_ref, kt_ref, q_ref, k_ref, v_ref, b_ref, o_ref, m_ref, acc_ref):
    step = pl.program_id(1)
    qi, ki = qt_ref[step], kt_ref[step]
    d = DSA_HEAD_DIM

    @pl.when(ki == 0)
    def _():
        m_ref[...] = jnp.full_like(m_ref, MASK_BIAS)
        acc_ref[...] = jnp.zeros_like(acc_ref)

    bias = b_ref[0]
    for h0 in range(0, DSA_HEADS, ATTN_HEAD_GROUP):
        heads = range(h0, h0 + ATTN_HEAD_GROUP)
        s = [_dot_nt(q_ref[0, :, h * d:(h + 1) * d], k_ref[0, :, h * d:(h + 1) * d]).astype(BF16) + bias
             for h in heads]
        m_old = [m_ref[h] for h in heads]
        m_new = [jnp.maximum(mo, jnp.max(x, axis=-1, keepdims=True).astype(F32)) for mo, x in zip(m_old, s)]
        alpha = [jnp.exp2(mo - mn) for mo, mn in zip(m_old, m_new)]
        p = [jnp.exp2(x - mn[:, 0:1].astype(BF16)) for x, mn in zip(s, m_new)]
        upd = [_dot(x, v_ref[0, :, 2 * h * d:2 * (h + 1) * d]) for x, h in zip(p, heads)]
        for i, h in enumerate(heads):
            acc_ref[h, :, :d] = alpha[i] * acc_ref[h, :, :d] + upd[i][:, :d]
            acc_ref[h, :, d:] = alpha[i] * acc_ref[h, :, d:] + upd[i][:, d:]
            m_ref[h] = m_new[i]

    @pl.when(ki == qi)
    def _():
        for h in range(DSA_HEADS):
            lo = h * d
            acc = acc_ref[h]
            o_ref[0, :, lo:lo + d] = (acc[:, :d] / acc[:, d:]).astype(o_ref.dtype)


def _dsa_attention(qb, kb, vx, bias, *, t):
    b, l, w = qb.shape
    n = l // t
    pairs = [(i, j) for i in range(n) for j in range(i + 1)]
    q_tile = jnp.asarray([p[0] for p in pairs], jnp.int32)
    k_tile = jnp.asarray([p[1] for p in pairs], jnp.int32)
    q_idx = lambda bi, s, qt, kt: (bi, qt[s], 0)
    kv_idx = lambda bi, s, qt, kt: (bi, kt[s], 0)
    return pl.pallas_call(
        _dsa_attn_body,
        grid_spec=pltpu.PrefetchScalarGridSpec(
            num_scalar_prefetch=2,
            grid=(b, len(pairs)),
            in_specs=[pl.BlockSpec((1, t, w), q_idx),
                      pl.BlockSpec((1, t, w), kv_idx),
                      pl.BlockSpec((1, t, 2 * w), kv_idx),
                      pl.BlockSpec((1, t, t), lambda bi, s, qt, kt: (bi, qt[s], kt[s]))],
            out_specs=pl.BlockSpec((1, t, w), q_idx),
            scratch_shapes=[pltpu.VMEM((DSA_HEADS, t, LANES), F32),
                            pltpu.VMEM((DSA_HEADS, t, 2 * DSA_HEAD_DIM), F32)]),
        out_shape=jax.ShapeDtypeStruct((b, l, w), BF16),
        compiler_params=_cparams("parallel", "arbitrary"),
        name="dsa_attention",
    )(q_tile, k_tile, qb, kb, vx, bias)


def _rope_tables(seq_len):
    def tab(dim):
        inv_freq = 1.0 / (ROPE_THETA ** (jnp.arange(0, dim, 2, dtype=F32) / dim))
        ang = jnp.arange(seq_len, dtype=F32)[:, None] * inv_freq[None, :]
        c, s = jnp.cos(ang), jnp.sin(ang)
        return jnp.concatenate([c, c], axis=1), jnp.concatenate([-s, s], axis=1)
    ca, sa = tab(DSA_HEAD_DIM)
    c64, s64 = tab(IDX_HEAD_DIM)
    ci = jnp.concatenate([c64, c64], axis=1)
    si = jnp.concatenate([s64, s64], axis=1)
    return ca, sa, ci, si


def _regroup_w_in(w):
    sizes = (3 * GDN_WIDTH, GDN_WIDTH, GDN_HEADS, GDN_HEADS, 3 * DSA_WIDTH,
             IDX_HEADS * IDX_HEAD_DIM, IDX_HEAD_DIM, IDX_HEADS, 1024, 1024)
    parts, start = [], 0
    for s in sizes:
        parts.append(w[:, start:start + s])
        start += s
    qkv_a, z_a, a_a, b_a, qkv_b, q_i, k_i, w_i, g_a, g_b = parts
    main = jnp.concatenate([qkv_a, z_a, qkv_b, g_a, g_b, q_i], axis=1).astype(BF16)
    pad = jnp.zeros((w.shape[0], SM_K - SM_W - IDX_HEADS), w.dtype)
    smalls = jnp.concatenate([a_a, b_a, w_i, pad, k_i], axis=1)
    return main, smalls


def kernel(x, norm_mix, w_in, conv_w, a_log, dt_bias, gdn_norm, w_out_gdn, w_out_dsa, w_o,
           norm_ffn, w_gate_up, w_down, norm_final):
    b, l, dm = x.shape
    depth = w_in.shape[0]
    m = b * l
    topk = min(TOPK_MAX, l // 4)
    tabs = _rope_tables(l)
    tm = min(1024, m)
    t_attn = min(512, l)
    tq_sel = min(256, l)
    h = x.reshape(m, dm)
    for layer in range(depth):
        w_main, w_smalls = _regroup_w_in(w_in[layer])
        proj = _norm_matmul(h, norm_mix[layer], w_main, tm=min(512, m), tn=MAIN_COLS // 4, out_dtype=BF16)
        smalls = _norm_matmul(h, norm_mix[layer], w_smalls, tm=tm, tn=LANES, out_dtype=F32, exact=True)
        proj = proj.reshape(b, l, MAIN_COLS)
        smalls = smalls.reshape(b, l, LANES)

        q_a, k_a, v_a = _gdn_prep(proj, conv_w[layer], tl=min(256, l))
        o_a = _gdn_chunk(q_a, k_a, v_a, proj, smalls, a_log[layer], dt_bias[layer], gdn_norm[layer],
                         tl=min(512, l))

        q_b, k_b, v_x, q_i, k_i = _dsa_prep(proj, smalls, tabs, tl=min(512, l))
        bias = _dsa_select(q_i, smalls, k_i, tq=tq_sel, tc=min(512, l), topk=topk)
        o_b = _dsa_attention(q_b, k_b, v_x, bias, t=t_attn)

        h = _mixer_out(o_a.reshape(m, GDN_WIDTH), o_b.reshape(m, DSA_WIDTH),
                       w_out_gdn[layer].astype(BF16), w_out_dsa[layer].astype(BF16), w_o[layer].astype(BF16),
                       proj.reshape(m, MAIN_COLS), h, tm=min(512, m))
        last = layer == depth - 1
        h = _ffn(h, norm_ffn[layer], w_gate_up[layer].astype(BF16), w_down[layer].astype(BF16),
                 tm=min(512, m), final_gain=norm_final if last else None)
    return h.reshape(b, l, dm)
```

```python
import functools
import math

import jax
import jax.numpy as jnp
from jax import lax
from jax.experimental import pallas as pl
from jax.experimental.pallas import tpu as pltpu

F32 = jnp.float32
BF16 = jnp.bfloat16
HIGHEST = lax.Precision.HIGHEST

GDN_HEADS = 8
GDN_HEAD_DIM = 128
GDN_WIDTH = GDN_HEADS * GDN_HEAD_DIM
CONV_K = 4
GDN_CHUNK = 64
DSA_HEADS = 8
DSA_HEAD_DIM = 128
DSA_WIDTH = DSA_HEADS * DSA_HEAD_DIM
IDX_HEADS = 8
IDX_HEAD_DIM = 64
TOPK_MAX = 256
ROPE_THETA = 10000.0
NORM_EPS = 1e-6

LANES = 128
VMEM_LIMIT = 48 * 1024 * 1024
BISECT_STEPS = 32
SELECT_ROW_BLOCK = 64
ATTN_HEAD_GROUP = 4
MASK_BIAS = -(2.0 ** 100)
Q_SCALE = DSA_HEAD_DIM ** -0.5 * math.log2(math.e)

COL_QKV_A = 0
COL_Z_A = 3 * GDN_WIDTH
COL_QKV_B = COL_Z_A + GDN_WIDTH
COL_G_A = COL_QKV_B + 3 * DSA_WIDTH
COL_Q_I = COL_G_A + 2 * 1024
MAIN_COLS = COL_Q_I + IDX_HEADS * IDX_HEAD_DIM
SM_A, SM_B, SM_W, SM_K = 0, 8, 16, 64


def _cparams(*sem):
    return pltpu.CompilerParams(dimension_semantics=sem, vmem_limit_bytes=VMEM_LIMIT)


def _rmsnorm_rows(x, gain):
    ms = jnp.mean(x * x, axis=-1, keepdims=True)
    return x * lax.rsqrt(ms + NORM_EPS) * gain


def _dot(a, b):
    return jnp.dot(a, b, preferred_element_type=F32)


def _dot_nt(a, b):
    return lax.dot_general(a, b, (((1,), (1,)), ((), ())), preferred_element_type=F32)


def _resident(shape):
    return pl.BlockSpec(shape, lambda i, j: (0,) * len(shape), pipeline_mode=pl.Buffered(1))


def _norm_mm_body(x_ref, g_ref, w_ref, o_ref, xn_ref, *, tn, exact):
    @pl.when(pl.program_id(1) == 0)
    def _():
        xn_ref[...] = _rmsnorm_rows(x_ref[...], g_ref[...]).astype(xn_ref.dtype)

    w = w_ref[:, pl.ds(pl.multiple_of(pl.program_id(1) * tn, tn), tn)]
    if exact:
        x = xn_ref[...]
        x_hi, w_hi = x.astype(BF16), w.astype(BF16)
        x_lo = (x - x_hi.astype(F32)).astype(BF16)
        w_lo = (w - w_hi.astype(F32)).astype(BF16)
        o = _dot(x_hi, w_hi) + (_dot(x_hi, w_lo) + _dot(x_lo, w_hi))
    else:
        o = _dot(xn_ref[...], w)
    o_ref[...] = o.astype(o_ref.dtype)


def _norm_matmul(x, gain, w, *, tm, tn, out_dtype, exact=False):
    m, k = x.shape
    n = w.shape[1]
    return pl.pallas_call(
        functools.partial(_norm_mm_body, tn=tn, exact=exact),
        grid=(m // tm, n // tn),
        in_specs=[pl.BlockSpec((tm, k), lambda i, j: (i, 0)),
                  _resident((1, k)),
                  _resident((k, n))],
        out_specs=pl.BlockSpec((tm, tn), lambda i, j: (i, j)),
        out_shape=jax.ShapeDtypeStruct((m, n), out_dtype),
        scratch_shapes=[pltpu.VMEM((tm, k), w.dtype)],
        compiler_params=_cparams("parallel", "arbitrary"),
        name="norm_matmul_exact" if exact else "norm_matmul",
    )(x, gain.reshape(1, k), w)


def _resident1(shape):
    return pl.BlockSpec(shape, lambda i: (0,) * len(shape), pipeline_mode=pl.Buffered(1))


def _ffn_body(x_ref, g_ref, wgu_ref, wd_ref, *rest, f, final_norm):
    if final_norm:
        gf_ref, o_ref = rest
    else:
        (o_ref,) = rest
    x = x_ref[...]
    xn = _rmsnorm_rows(x, g_ref[...]).astype(BF16)
    g = _dot(xn, wgu_ref[:, :f])
    u = _dot(xn, wgu_ref[:, f:])
    act = (g * jax.nn.sigmoid(g) * u).astype(BF16)
    h = x + _dot(act, wd_ref[...])
    if final_norm:
        h = _rmsnorm_rows(h, gf_ref[...])
    o_ref[...] = h


def _ffn(x, gain, w_gate_up, w_down, *, tm, final_gain=None):
    m, k = x.shape
    f = w_down.shape[0]
    final_norm = final_gain is not None
    in_specs = [pl.BlockSpec((tm, k), lambda i: (i, 0)),
                _resident1((1, k)), _resident1((k, 2 * f)), _resident1((f, k))]
    args = [x, gain.reshape(1, k), w_gate_up, w_down]
    if final_norm:
        in_specs.append(_resident1((1, k)))
        args.append(final_gain.reshape(1, k))
    return pl.pallas_call(
        functools.partial(_ffn_body, f=f, final_norm=final_norm),
        grid=(m // tm,),
        in_specs=in_specs,
        out_specs=pl.BlockSpec((tm, k), lambda i: (i, 0)),
        out_shape=jax.ShapeDtypeStruct((m, k), F32),
        compiler_params=_cparams("parallel"),
        name="ffn_norm" if final_norm else "ffn",
    )(*args)


def _mixer_out_body(oa_ref, ob_ref, wa_ref, wb_ref, wo_ref, ga_ref, gb_ref, h_ref, o_ref):
    ya = _dot(oa_ref[...], wa_ref[...])
    yb = _dot(ob_ref[...], wb_ref[...])
    ga = jax.nn.sigmoid(ga_ref[...].astype(F32))
    gb = jax.nn.sigmoid(gb_ref[...].astype(F32))
    merged = (ga * ya + gb * yb).astype(BF16)
    o_ref[...] = h_ref[...] + _dot(merged, wo_ref[...])


def _mixer_out(oa, ob, wa, wb, wo, proj, h, *, tm):
    m, k = oa.shape
    n = wa.shape[1]
    row = lambda w: pl.BlockSpec((tm, w), lambda i: (i, 0))
    return pl.pallas_call(
        _mixer_out_body,
        grid=(m // tm,),
        in_specs=[row(k), row(k), _resident1((k, n)), _resident1((k, n)), _resident1((n, n)),
                  pl.BlockSpec((tm, n), lambda i: (i, COL_G_A // n)),
                  pl.BlockSpec((tm, n), lambda i: (i, COL_G_A // n + 1)),
                  row(n)],
        out_specs=row(n),
        out_shape=jax.ShapeDtypeStruct((m, n), F32),
        compiler_params=_cparams("parallel"),
        name="mixer_out",
    )(oa, ob, wa, wb, wo, proj, proj, h)


HALO = 16
CONV_ROWS = 128


def _gdn_prep_body(x_ref, halo_ref, cw_ref, q_ref, k_ref, v_ref, xs_ref, *, tl):
    i = pl.program_id(1)
    sub = CONV_ROWS
    xs_ref[0:HALO, :] = jnp.where(i == 0, jnp.zeros_like(halo_ref[0]), halo_ref[0])
    xs_ref[HALO:HALO + tl, :] = x_ref[0]
    t = lax.broadcasted_iota(jnp.int32, ((CONV_K - 1) * sub, HALO + sub), 0)
    c = lax.broadcasted_iota(jnp.int32, ((CONV_K - 1) * sub, HALO + sub), 1)
    shift = jnp.where(c == t % sub + HALO - (CONV_K - 1) + t // sub, 1.0, 0.0).astype(BF16)
    for r0 in range(0, tl, sub):
        win = xs_ref[r0:r0 + HALO + sub, :]
        z = _dot(shift, win)
        y = cw_ref[CONV_K - 1:CONV_K, :] * win[HALO:, :].astype(F32)
        for j in range(CONV_K - 1):
            y = y + cw_ref[j:j + 1, :] * z[j * sub:(j + 1) * sub]
        y = y * jax.nn.sigmoid(y)
        rows = slice(r0, r0 + sub)
        for h in range(GDN_HEADS):
            lo = h * GDN_HEAD_DIM
            qh = y[:, lo:lo + GDN_HEAD_DIM]
            kh = y[:, GDN_WIDTH + lo:GDN_WIDTH + lo + GDN_HEAD_DIM]
            qn = qh * lax.rsqrt(jnp.sum(qh * qh, axis=-1, keepdims=True) + NORM_EPS)
            kn = kh * lax.rsqrt(jnp.sum(kh * kh, axis=-1, keepdims=True) + NORM_EPS)
            q_ref[0, rows, lo:lo + GDN_HEAD_DIM] = (qn * (GDN_HEAD_DIM ** -0.5)).astype(q_ref.dtype)
            k_ref[0, rows, lo:lo + GDN_HEAD_DIM] = kn.astype(k_ref.dtype)
        v_ref[0, rows, :] = y[:, 2 * GDN_WIDTH:].astype(v_ref.dtype)


def _gdn_prep(proj, conv_w, *, tl):
    b, l, _ = proj.shape
    c = 3 * GDN_WIDTH
    r = tl // HALO
    out = jax.ShapeDtypeStruct((b, l, GDN_WIDTH), BF16)
    ospec = pl.BlockSpec((1, tl, GDN_WIDTH), lambda bi, i: (bi, i, 0))
    return pl.pallas_call(
        functools.partial(_gdn_prep_body, tl=tl),
        grid=(b, l // tl),
        in_specs=[pl.BlockSpec((1, tl, c), lambda bi, i: (bi, i, 0)),
                  pl.BlockSpec((1, HALO, c), lambda bi, i: (bi, jnp.maximum(i * r - 1, 0), 0)),
                  pl.BlockSpec((CONV_K, c), lambda bi, i: (0, 0))],
        out_specs=[ospec, ospec, ospec],
        out_shape=[out, out, out],
        scratch_shapes=[pltpu.VMEM((HALO + tl, c), BF16)],
        compiler_params=_cparams("parallel", "parallel"),
        name="gdn_prep",
    )(proj, proj, conv_w)


def _gdn_chunk_body(q_ref, k_ref, v_ref, z_ref, sm_ref, alog_ref, dtb_ref, gn_ref, o_ref,
                    s_ref, wq_ref, qkk_ref, u_ref, gl_ref, *, nch, nb):
    c = GDN_CHUNK
    d = GDN_HEAD_DIM
    probs = [(bb, h) for bb in range(nb) for h in range(GDN_HEADS)]

    @pl.when(pl.program_id(1) == 0)
    def _():
        s_ref[...] = jnp.zeros_like(s_ref)

    row = lax.broadcasted_iota(jnp.int32, (c, c), 0)
    col = lax.broadcasted_iota(jnp.int32, (c, c), 1)
    tril_incl = row >= col
    tril_strict = row > col
    eye = (row == col).astype(F32)
    tri_f = tril_incl.astype(F32)
    triu_f = (row <= col).astype(F32)

    def intra(ci, carry):
        r0 = pl.multiple_of(ci * c, c)
        gam_col, gam_row, beta_all = [], [], []
        for bb in range(nb):
            sm = sm_ref[bb, pl.ds(r0, c), :]
            g_all = -jnp.exp(alog_ref[...]) * jax.nn.softplus(sm + dtb_ref[...])
            beta_all.append(jax.nn.sigmoid(sm))
            gam_col.append(jnp.dot(tri_f, g_all, precision=HIGHEST, preferred_element_type=F32))
            gam_row.append(jnp.dot(g_all.T, triu_f, precision=HIGHEST, preferred_element_type=F32))
        gc = [gam_col[bb][:, SM_A + h:SM_A + h + 1] for bb, h in probs]
        bc = [beta_all[bb][:, SM_B + h:SM_B + h + 1] for bb, h in probs]
        g_last = [g[c - 1:c, :] for g in gc]
        decay = [jnp.exp(jnp.minimum(g - gam_row[bb][SM_A + h:SM_A + h + 1, :], 0.0))
                 for g, (bb, h) in zip(gc, probs)]
        qh = [q_ref[bb, pl.ds(r0, c), h * d:(h + 1) * d] for bb, h in probs]
        kh = [k_ref[bb, pl.ds(r0, c), h * d:(h + 1) * d] for bb, h in probs]
        vh = [v_ref[bb, pl.ds(r0, c), h * d:(h + 1) * d] for bb, h in probs]
        kq = [_dot_nt(jnp.concatenate([k, q], axis=0), k) for k, q in zip(kh, qh)]
        qk = [jnp.where(tril_incl, x[c:] * dc, 0.0) for x, dc in zip(kq, decay)]
        n_pow = [-jnp.where(tril_strict, b_ * x[:c] * dc, 0.0) for x, b_, dc in zip(kq, bc, decay)]
        t_inv = [eye + n for n in n_pow]
        n_pow = [_dot(n.astype(BF16), n.astype(BF16)) for n in n_pow]
        for _ in range(int(math.log2(c)) - 2):
            prod = [_dot(jnp.concatenate([n, t], axis=0).astype(BF16), n.astype(BF16))
                    for n, t in zip(n_pow, t_inv)]
            n_pow = [p[:c] for p in prod]
            t_inv = [t + p[c:] for t, p in zip(t_inv, prod)]
        t_inv = [t + _dot(t.astype(BF16), n.astype(BF16)) for n, t in zip(n_pow, t_inv)]
        e_gc = [jnp.exp(g) for g in gc]
        kf = [k.astype(F32) for k in kh]
        rhs = [jnp.concatenate([v.astype(F32) * b_, k * (b_ * e)], axis=1)
               for v, k, b_, e in zip(vh, kf, bc, e_gc)]
        uw = [_dot(t.astype(BF16), r.astype(BF16)) for t, r in zip(t_inv, rhs)]
        for i, (bb, h) in enumerate(probs):
            k_dec = kf[i] * jnp.exp(g_last[i] - gc[i])
            u_ref[bb, ci, h] = uw[i][:, :d]
            wq_ref[bb, ci, h] = jnp.concatenate([uw[i][:, d:], qh[i].astype(F32) * e_gc[i]], axis=0).astype(BF16)
            qkk_ref[bb, ci, h] = jnp.concatenate([qk[i], k_dec.T], axis=0).astype(BF16)
            gl_ref[bb, ci, h] = jnp.broadcast_to(jnp.exp(g_last[i]), (1, d))
        return carry

    lax.fori_loop(0, nch, intra, 0)

    def scan(ci, carry):
        r0 = pl.multiple_of(ci * c, c)
        s_old = [s_ref[bb, h] for bb, h in probs]
        m1 = [_dot(wq_ref[bb, ci, h], s.astype(BF16)) for s, (bb, h) in zip(s_old, probs)]
        v_new = [u_ref[bb, ci, h] - m[:c] for m, (bb, h) in zip(m1, probs)]
        m2 = [_dot(qkk_ref[bb, ci, h], v.astype(BF16)) for v, (bb, h) in zip(v_new, probs)]
        for i, (bb, h) in enumerate(probs):
            s_ref[bb, h] = s_old[i] * gl_ref[bb, ci, h] + m2[i][c:]
            o = m1[i][c:] + m2[i][:c]
            o = o * lax.rsqrt(jnp.mean(o * o, axis=-1, keepdims=True) + NORM_EPS) * gn_ref[...]
            z = z_ref[bb, pl.ds(r0, c), h * d:(h + 1) * d].astype(F32)
            o_ref[bb, pl.ds(r0, c), h * d:(h + 1) * d] = (o * (z * jax.nn.sigmoid(z))).astype(o_ref.dtype)
        return carry

    lax.fori_loop(0, nch, scan, 0)


def _gdn_chunk(q, k, v, proj, smalls, a_log, dt_bias, gdn_norm, *, tl):
    b, l, _ = q.shape
    c, d, nh = GDN_CHUNK, GDN_HEAD_DIM, GDN_HEADS
    nch = tl // c
    nb = 2 if b % 2 == 0 else 1
    pad = lambda t, at: jnp.zeros((1, LANES), F32).at[0, at:at + t.shape[0]].set(t.astype(F32))
    qspec = pl.BlockSpec((nb, tl, GDN_WIDTH), lambda bi, i: (bi, i, 0))
    vec = pl.BlockSpec((1, LANES), lambda bi, i: (0, 0))
    return pl.pallas_call(
        functools.partial(_gdn_chunk_body, nch=nch, nb=nb),
        grid=(b // nb, l // tl),
        in_specs=[qspec, qspec, qspec,
                  pl.BlockSpec((nb, tl, GDN_WIDTH), lambda bi, i: (bi, i, COL_Z_A // GDN_WIDTH)),
                  pl.BlockSpec((nb, tl, LANES), lambda bi, i: (bi, i, 0)),
                  vec, vec, vec],
        out_specs=qspec,
        out_shape=jax.ShapeDtypeStruct((b, l, GDN_WIDTH), BF16),
        scratch_shapes=[pltpu.VMEM((nb, nh, d, d), F32),
                        pltpu.VMEM((nb, nch, nh, 2 * c, d), BF16),
                        pltpu.VMEM((nb, nch, nh, c + d, c), BF16),
                        pltpu.VMEM((nb, nch, nh, c, d), F32),
                        pltpu.VMEM((nb, nch, nh, 1, d), F32)],
        compiler_params=_cparams("parallel", "arbitrary"),
        name="gdn_chunk",
    )(q, k, v, proj, smalls, pad(a_log, SM_A), pad(dt_bias, SM_A), gdn_norm.reshape(1, LANES).astype(F32))


def _swap_halves(x, half):
    if 2 * half == LANES:
        return pltpu.roll(x, half, axis=1)
    lane = lax.broadcasted_iota(jnp.int32, x.shape, 1)
    first = (lane % (2 * half)) < half
    return jnp.where(first, pltpu.roll(x, LANES - half, axis=1), pltpu.roll(x, half, axis=1))


def _dsa_prep_body(qk_ref, v_ref, qi_ref, sm_ref, ca_ref, sa_ref, ci_ref, si_ref,
                   qb_ref, kb_ref, vx_ref, qio_ref, kio_ref):
    ca, sa = ca_ref[...], sa_ref[...]
    ones = jnp.ones((v_ref.shape[1], DSA_HEAD_DIM), BF16)
    for h in range(DSA_HEADS):
        lo = h * DSA_HEAD_DIM
        vx_ref[0, :, 2 * lo:2 * lo + DSA_HEAD_DIM] = v_ref[0, :, lo:lo + DSA_HEAD_DIM]
        vx_ref[0, :, 2 * lo + DSA_HEAD_DIM:2 * lo + 2 * DSA_HEAD_DIM] = ones
    for h in range(2 * DSA_HEADS):
        lo = h * DSA_HEAD_DIM
        x = qk_ref[0, :, lo:lo + DSA_HEAD_DIM].astype(F32)
        y = x * ca + _swap_halves(x, DSA_HEAD_DIM // 2) * sa
        if h < DSA_HEADS:
            qb_ref[0, :, lo:lo + DSA_HEAD_DIM] = (y * Q_SCALE).astype(BF16)
        else:
            lo -= DSA_WIDTH
            kb_ref[0, :, lo:lo + DSA_HEAD_DIM] = y.astype(BF16)
    ci, si = ci_ref[...], si_ref[...]
    for p in range(IDX_HEADS * IDX_HEAD_DIM // LANES):
        lo = p * LANES
        x = qi_ref[0, :, lo:lo + LANES].astype(F32)
        y = x * ci + _swap_halves(x, IDX_HEAD_DIM // 2) * si
        qio_ref[0, :, lo:lo + LANES] = y.astype(BF16)
    x = sm_ref[0]
    y = x * ci + _swap_halves(x, IDX_HEAD_DIM // 2) * si
    kio_ref[0] = (y[:, SM_K:] * (IDX_HEAD_DIM ** -0.5)).astype(BF16)


def _dsa_prep(proj, smalls, tabs, *, tl):
    b, l, _ = proj.shape
    w2 = 2 * DSA_WIDTH
    wi = IDX_HEADS * IDX_HEAD_DIM
    tab = pl.BlockSpec((tl, LANES), lambda bi, i: (i, 0))
    big = lambda n: pl.BlockSpec((1, tl, n), lambda bi, i: (bi, i, 0))
    return pl.pallas_call(
        _dsa_prep_body,
        grid=(b, l // tl),
        in_specs=[pl.BlockSpec((1, tl, w2), lambda bi, i: (bi, i, COL_QKV_B // w2)),
                  pl.BlockSpec((1, tl, DSA_WIDTH), lambda bi, i: (bi, i, (COL_QKV_B + w2) // DSA_WIDTH)),
                  pl.BlockSpec((1, tl, wi), lambda bi, i: (bi, i, COL_Q_I // wi)),
                  big(LANES), tab, tab, tab, tab],
        out_specs=[big(DSA_WIDTH), big(DSA_WIDTH), big(w2), big(wi), big(IDX_HEAD_DIM)],
        out_shape=[jax.ShapeDtypeStruct((b, l, DSA_WIDTH), BF16),
                   jax.ShapeDtypeStruct((b, l, DSA_WIDTH), BF16),
                   jax.ShapeDtypeStruct((b, l, w2), BF16),
                   jax.ShapeDtypeStruct((b, l, wi), BF16),
                   jax.ShapeDtypeStruct((b, l, IDX_HEAD_DIM), BF16)],
        compiler_params=_cparams("parallel", "parallel"),
        name="dsa_prep",
    )(proj, proj, proj, smalls, *tabs)


def _dsa_select_body(qi_ref, sm_ref, ki_ref, bias_in_ref, o_ref, s_ref, *, tq, tc, topk, n_c, tile0):
    del bias_in_ref
    nl = tc // LANES
    half = tq // 2
    q0 = (tile0 + pl.program_id(1)) * tq
    kf = float(topk)
    w_all = sm_ref[0] * (IDX_HEADS ** -0.5)
    rb = min(SELECT_ROW_BLOCK, half)
    rep = lambda col: jnp.broadcast_to(col, (col.shape[0], LANES))
    lane_tiles = lambda x: [x[:, j * LANES:(j + 1) * LANES] for j in range(nl)]

    def score_chunk(k0, carry, last):
        mn, mx, n_pos, n_nonneg = carry
        kc = ki_ref[0, pl.ds(k0, tc), :]
        s = jnp.zeros((tq, tc), F32)
        for h in range(IDX_HEADS):
            qh = qi_ref[0, :, h * IDX_HEAD_DIM:(h + 1) * IDX_HEAD_DIM]
            s = s + w_all[:, SM_W + h:SM_W + h + 1] * jnp.maximum(_dot_nt(qh, kc), 0.0)
        if last:
            qpos = q0 + lax.broadcasted_iota(jnp.int32, (tq, tc), 0)
            causal = k0 + lax.broadcasted_iota(jnp.int32, (tq, tc), 1) <= qpos
            s_lo, s_hi = jnp.where(causal, s, -jnp.inf), jnp.where(causal, s, jnp.inf)
        else:
            s_lo = s_hi = s
        s_ref[:, pl.ds(k0, tc)] = s_lo
        for lo_t, hi_t in zip(lane_tiles(s_lo), lane_tiles(s_hi)):
            mx = jnp.maximum(mx, lo_t)
            mn = jnp.minimum(mn, hi_t)
            n_pos = n_pos + jnp.where(lo_t > 0.0, 1.0, 0.0)
            n_nonneg = n_nonneg + jnp.where(lo_t >= 0.0, 1.0, 0.0)
        return mn, mx, n_pos, n_nonneg

    zeros = jnp.zeros((tq, LANES), F32)
    stats = lax.fori_loop(0, n_c - 1, lambda ci, c: score_chunk(pl.multiple_of(ci * tc, tc), c, False),
                          (zeros + jnp.inf, zeros - jnp.inf, zeros, zeros), unroll=2)
    mn, mx, n_pos, n_nonneg = score_chunk((n_c - 1) * tc, stats, True)
    row_min = rep(jnp.min(mn, axis=1, keepdims=True))
    row_max = rep(jnp.max(mx, axis=1, keepdims=True))
    n_pos = rep(jnp.sum(n_pos, axis=1, keepdims=True))
    n_nonneg = rep(jnp.sum(n_nonneg, axis=1, keepdims=True))

    def reduce_keys(fn, init, combine, r_lo, r_hi, unrolled):
        accs = []
        for r0 in range(r_lo, r_hi, rb):
            def chunk(k0, acc, r0=r0):
                for j in range(nl):
                    acc = combine(acc, fn(s_ref[r0:r0 + rb, pl.ds(k0 + j * LANES, LANES)], r0 - r_lo))
                return acc
            acc = jnp.full((rb, LANES), init, F32)
            if unrolled:
                for ci in range(n_c):
                    acc = chunk(ci * tc, acc)
            else:
                acc = lax.fori_loop(0, n_c, lambda ci, a: chunk(pl.multiple_of(ci * tc, tc), a), acc)
            accs.append(acc)
        return jnp.concatenate(accs, axis=0)

    def count_acc(op, th, r_lo, r_hi, unrolled=True):
        return reduce_keys(lambda t, r: jnp.where(op(t, th[r:r + rb]), 1.0, 0.0), 0.0, jnp.add, r_lo, r_hi, unrolled)

    lane_max = lambda x: rep(jnp.max(x, axis=1, keepdims=True))
    lane_sum = lambda x: rep(jnp.sum(x, axis=1, keepdims=True))
    ge = lambda a, b: a >= b
    count = lambda op, th: lane_sum(count_acc(op, th, 0, tq, unrolled=False))
    is_open = lambda lo, hi: jnp.max(hi - lo) > 0.0
    middle = lambda lo, hi: lo + (hi - lo) * 0.5

    def narrow(lo, hi, mid, cnt):
        hit = cnt == kf
        above = cnt > kf
        return jnp.where(hit | above, mid, lo), jnp.where(hit | jnp.logical_not(above), mid, hi)

    n_causal = (q0 + 1 + lax.broadcasted_iota(jnp.int32, (tq, LANES), 0)).astype(F32)
    few = n_causal <= kf
    zero_tied = (n_pos < kf) & (n_nonneg >= kf)
    starts_closed = few | zero_tied
    closed_at = jnp.where(few, -jnp.finfo(F32).max, 0.0)
    positive = n_pos >= kf
    lo = jnp.where(starts_closed, closed_at, jnp.where(positive, jnp.maximum(row_min, 0.0), row_min))
    hi = jnp.where(starts_closed, closed_at, jnp.where(positive, row_max, jnp.minimum(row_max, 0.0)))
    surplus = jnp.where((zero_tied & (n_nonneg > kf)) | (jnp.logical_not(starts_closed) & (lo >= hi)), 1.0, 0.0)

    def search_cond(c):
        return (c[0] < BISECT_STEPS) & (c[7] > 0)

    def search_body(c):
        it, lo_a, hi_a, lo_b, hi_b, acc_b, mid_b, _ = c
        go_on = (is_open(lo_a, hi_a) | is_open(lo_b, hi_b)).astype(jnp.int32)
        mid_a = middle(lo_a, hi_a)
        acc_a = count_acc(ge, mid_a, 0, half)
        lo_b, hi_b = narrow(lo_b, hi_b, mid_b, lane_sum(acc_b))
        mid_b = middle(lo_b, hi_b)
        acc_b = count_acc(ge, mid_b, half, tq)
        lo_a, hi_a = narrow(lo_a, hi_a, mid_a, lane_sum(acc_a))
        return it + 1, lo_a, hi_a, lo_b, hi_b, acc_b, mid_b, go_on

    lo_a, hi_a, lo_b, hi_b = lo[:half], hi[:half], lo[half:], hi[half:]
    mid_b = middle(lo_b, hi_b)
    _, lo_a, hi_a, lo_b, hi_b, acc_b, mid_b, _ = lax.while_loop(
        search_cond, search_body,
        (jnp.int32(0), lo_a, hi_a, lo_b, hi_b, count_acc(ge, mid_b, half, tq), mid_b, jnp.int32(1)))
    lo_b, hi_b = narrow(lo_b, hi_b, mid_b, lane_sum(acc_b))
    lo = jnp.concatenate([lo_a, lo_b], axis=0)
    hi = jnp.concatenate([hi_a, hi_b], axis=0)

    def peel_cond(c):
        return (c[0] < topk + 2) & (c[4] > 0)

    def peel_body(c):
        it, lo, hi, surplus, _ = c
        still = hi > lo
        widen = jnp.where(it == 0, jnp.finfo(F32).max, 0.0)
        below = jnp.where(hi == row_max, hi + widen, hi)
        top = lane_max(reduce_keys(
            lambda t, r: jnp.where((t >= lo[r:r + rb]) & (t < below[r:r + rb]), t, -jnp.inf),
            -jnp.inf, jnp.maximum, 0, tq, False))
        n_top = count(ge, top)
        found = still & (n_top >= kf)
        lo = jnp.where(found, top, lo)
        hi = jnp.where(still, top, hi)
        surplus = jnp.where(found & (n_top > kf), 1.0, surplus)
        return it + 1, lo, hi, surplus, is_open(lo, hi).astype(jnp.int32)

    _, thr, _, surplus, _ = lax.while_loop(
        peel_cond, peel_body, (jnp.int32(0), lo, hi, surplus, is_open(lo, hi).astype(jnp.int32)))

    for r0 in range(0, tq, rb):
        rows = slice(r0, r0 + rb)
        thr_r = thr[rows]
        has_excess = jnp.max(surplus[rows]) > 0.0

        def write_plain(ci, carry, rows=rows, thr_r=thr_r):
            k0 = pl.multiple_of(ci * tc, tc)
            for j in range(nl):
                cols = pl.ds(k0 + j * LANES, LANES)
                o_ref[0, rows, cols] = jnp.where(s_ref[rows, cols] >= thr_r, 0.0, MASK_BIAS).astype(o_ref.dtype)
            return carry

        @pl.when(jnp.logical_not(has_excess))
        def _(write_plain=write_plain):
            lax.fori_loop(0, n_c, write_plain, 0)

        @pl.when(has_excess)
        def _(r0=r0, rows=rows, thr_r=thr_r):
            n_gt = lane_sum(count_acc(lambda a, b: a > b, thr_r, r0, r0 + rb, unrolled=False))
            need_c = (kf - n_gt)[:, 0:1]
            thr_c = thr_r[:, 0:1]
            r = lax.broadcasted_iota(jnp.int32, (tc, tc), 0)
            cidx = lax.broadcasted_iota(jnp.int32, (tc, tc), 1)
            upper = jnp.where(r < cidx, 1.0, 0.0).astype(BF16)

            def write_ties(ci, seen):
                k0 = pl.multiple_of(ci * tc, tc)
                sc = s_ref[rows, pl.ds(k0, tc)]
                eq = sc == thr_c
                before = seen + _dot(jnp.where(eq, 1.0, 0.0).astype(BF16), upper)
                keep = (sc > thr_c) | (eq & (before < need_c))
                o_ref[0, rows, pl.ds(k0, tc)] = jnp.where(keep, 0.0, MASK_BIAS).astype(o_ref.dtype)
                return seen + jnp.sum(jnp.where(eq, 1.0, 0.0), axis=1, keepdims=True)

            lax.fori_loop(0, n_c, write_ties, jnp.zeros((rb, 1), F32))


def _dsa_select(qi, smalls, ki, *, tq, tc, topk):
    b, l, wi = qi.shape
    per_call = tc // tq
    bias = jnp.full((b, l, l), MASK_BIAS, BF16)
    for n_c in range(1, l // tc + 1):
        tile0 = (n_c - 1) * per_call
        bias = pl.pallas_call(
            functools.partial(_dsa_select_body, tq=tq, tc=tc, topk=topk, n_c=n_c, tile0=tile0),
            grid=(b, per_call),
            in_specs=[pl.BlockSpec((1, tq, wi), lambda bi, t, tile0=tile0: (bi, tile0 + t, 0)),
                      pl.BlockSpec((1, tq, LANES), lambda bi, t, tile0=tile0: (bi, tile0 + t, 0)),
                      pl.BlockSpec((1, n_c * tc, IDX_HEAD_DIM), lambda bi, t: (bi, 0, 0)),
                      pl.BlockSpec(memory_space=pl.ANY)],
            out_specs=pl.BlockSpec((1, tq, n_c * tc), lambda bi, t, tile0=tile0: (bi, tile0 + t, 0)),
            out_shape=jax.ShapeDtypeStruct((b, l, l), BF16),
            scratch_shapes=[pltpu.VMEM((tq, n_c * tc), F32)],
            input_output_aliases={3: 0},
            compiler_params=_cparams("parallel", "parallel"),
            name=f"dsa_select_{n_c}",
        )(qi, smalls, ki, bias)
    return bias


def _dsa_attn_body(qt_ref, kt_ref, q_ref, k_ref, v_ref, b_ref, o_ref, m_ref, acc_ref):
    step = pl.program_id(1)
    qi, ki = qt_ref[step], kt_ref[step]
    d = DSA_HEAD_DIM

    @pl.when(ki == 0)
    def _():
        m_ref[...] = jnp.full_like(m_ref, MASK_BIAS)
        acc_ref[...] = jnp.zeros_like(acc_ref)

    bias = b_ref[0]
    for h0 in range(0, DSA_HEADS, ATTN_HEAD_GROUP):
        heads = range(h0, h0 + ATTN_HEAD_GROUP)
        s = [_dot_nt(q_ref[0, :, h * d:(h + 1) * d], k_ref[0, :, h * d:(h + 1) * d]).astype(BF16) + bias
             for h in heads]
        m_old = [m_ref[h] for h in heads]
        m_new = [jnp.maximum(mo, jnp.max(x, axis=-1, keepdims=True).astype(F32)) for mo, x in zip(m_old, s)]
        alpha = [jnp.exp2(mo - mn) for mo, mn in zip(m_old, m_new)]
        p = [jnp.exp2(x - mn[:, 0:1].astype(BF16)) for x, mn in zip(s, m_new)]
        upd = [_dot(x, v_ref[0, :, 2 * h * d:2 * (h + 1) * d]) for x, h in zip(p, heads)]
        for i, h in enumerate(heads):
            acc_ref[h, :, :d] = alpha[i] * acc_ref[h, :, :d] + upd[i][:, :d]
            acc_ref[h, :, d:] = alpha[i] * acc_ref[h, :, d:] + upd[i][:, d:]
            m_ref[h] = m_new[i]

    @pl.when(ki == qi)
    def _():
        for h in range(DSA_HEADS):
            lo = h * d
            acc = acc_ref[h]
            o_ref[0, :, lo:lo + d] = (acc[:, :d] / acc[:, d:]).astype(o_ref.dtype)


def _dsa_attention(qb, kb, vx, bias, *, t):
    b, l, w = qb.shape
    n = l // t
    pairs = [(i, j) for i in range(n) for j in range(i + 1)]
    q_tile = jnp.asarray([p[0] for p in pairs], jnp.int32)
    k_tile = jnp.asarray([p[1] for p in pairs], jnp.int32)
    q_idx = lambda bi, s, qt, kt: (bi, qt[s], 0)
    kv_idx = lambda bi, s, qt, kt: (bi, kt[s], 0)
    return pl.pallas_call(
        _dsa_attn_body,
        grid_spec=pltpu.PrefetchScalarGridSpec(
            num_scalar_prefetch=2,
            grid=(b, len(pairs)),
            in_specs=[pl.BlockSpec((1, t, w), q_idx),
                      pl.BlockSpec((1, t, w), kv_idx),
                      pl.BlockSpec((1, t, 2 * w), kv_idx),
                      pl.BlockSpec((1, t, t), lambda bi, s, qt, kt: (bi, qt[s], kt[s]))],
            out_specs=pl.BlockSpec((1, t, w), q_idx),
            scratch_shapes=[pltpu.VMEM((DSA_HEADS, t, LANES), F32),
                            pltpu.VMEM((DSA_HEADS, t, 2 * DSA_HEAD_DIM), F32)]),
        out_shape=jax.ShapeDtypeStruct((b, l, w), BF16),
        compiler_params=_cparams("parallel", "arbitrary"),
        name="dsa_attention",
    )(q_tile, k_tile, qb, kb, vx, bias)


def _rope_tables(seq_len):
    def tab(dim):
        inv_freq = 1.0 / (ROPE_THETA ** (jnp.arange(0, dim, 2, dtype=F32) / dim))
        ang = jnp.arange(seq_len, dtype=F32)[:, None] * inv_freq[None, :]
        c, s = jnp.cos(ang), jnp.sin(ang)
        return jnp.concatenate([c, c], axis=1), jnp.concatenate([-s, s], axis=1)
    ca, sa = tab(DSA_HEAD_DIM)
    c64, s64 = tab(IDX_HEAD_DIM)
    ci = jnp.concatenate([c64, c64], axis=1)
    si = jnp.concatenate([s64, s64], axis=1)
    return ca, sa, ci, si


def _regroup_w_in(w):
    sizes = (3 * GDN_WIDTH, GDN_WIDTH, GDN_HEADS, GDN_HEADS, 3 * DSA_WIDTH,
             IDX_HEADS * IDX_HEAD_DIM, IDX_HEAD_DIM, IDX_HEADS, 1024, 1024)
    parts, start = [], 0
    for s in sizes:
        parts.append(w[:, start:start + s])
        start += s
    qkv_a, z_a, a_a, b_a, qkv_b, q_i, k_i, w_i, g_a, g_b = parts
    main = jnp.concatenate([qkv_a, z_a, qkv_b, g_a, g_b, q_i], axis=1).astype(BF16)
    pad = jnp.zeros((w.shape[0], SM_K - SM_W - IDX_HEADS), w.dtype)
    smalls = jnp.concatenate([a_a, b_a, w_i, pad, k_i], axis=1)
    return main, smalls


def kernel(x, norm_mix, w_in, conv_w, a_log, dt_bias, gdn_norm, w_out_gdn, w_out_dsa, w_o,
           norm_ffn, w_gate_up, w_down, norm_final):
    b, l, dm = x.shape
    depth = w_in.shape[0]
    m = b * l
    topk = min(TOPK_MAX, l // 4)
    tabs = _rope_tables(l)
    tm = min(1024, m)
    t_attn = min(512, l)
    tq_sel = min(256, l)
    h = x.reshape(m, dm)
    for layer in range(depth):
        w_main, w_smalls = _regroup_w_in(w_in[layer])
        proj = _norm_matmul(h, norm_mix[layer], w_main, tm=min(512, m), tn=MAIN_COLS // 4, out_dtype=BF16)
        smalls = _norm_matmul(h, norm_mix[layer], w_smalls, tm=tm, tn=LANES, out_dtype=F32, exact=True)
        proj = proj.reshape(b, l, MAIN_COLS)
        smalls = smalls.reshape(b, l, LANES)

        q_a, k_a, v_a = _gdn_prep(proj, conv_w[layer], tl=min(256, l))
        o_a = _gdn_chunk(q_a, k_a, v_a, proj, smalls, a_log[layer], dt_bias[layer], gdn_norm[layer],
                         tl=min(512, l))

        q_b, k_b, v_x, q_i, k_i = _dsa_prep(proj, smalls, tabs, tl=min(512, l))
        bias = _dsa_select(q_i, smalls, k_i, tq=tq_sel, tc=min(512, l), topk=topk)
        o_b = _dsa_attention(q_b, k_b, v_x, bias, t=t_attn)

        h = _mixer_out(o_a.reshape(m, GDN_WIDTH), o_b.reshape(m, DSA_WIDTH),
                       w_out_gdn[layer].astype(BF16), w_out_dsa[layer].astype(BF16), w_o[layer].astype(BF16),
                       proj.reshape(m, MAIN_COLS), h, tm=min(512, m))
        last = layer == depth - 1
        h = _ffn(h, norm_ffn[layer], w_gate_up[layer].astype(BF16), w_down[layer].astype(BF16),
                 tm=min(512, m), final_gain=norm_final if last else None)
    return h.reshape(b, l, dm)
```

```python
import functools
import math

import jax
import jax.numpy as jnp
from jax import lax
from jax.experimental import pallas as pl
from jax.experimental.pallas import tpu as pltpu

F32 = jnp.float32
BF16 = jnp.bfloat16

GDN_HEADS = 8
GDN_HEAD_DIM = 128
GDN_WIDTH = GDN_HEADS * GDN_HEAD_DIM
CONV_K = 4
GDN_CHUNK = 64
DSA_HEADS = 8
DSA_HEAD_DIM = 128
DSA_WIDTH = DSA_HEADS * DSA_HEAD_DIM
IDX_HEADS = 8
IDX_HEAD_DIM = 64
TOPK_MAX = 256
ROPE_THETA = 10000.0
ROPE_SPLIT = 64
NORM_EPS = 1e-6

LANES = 128
VMEM_LIMIT = 48 * 1024 * 1024
BISECT_STEPS = 32
SELECT_ROW_BLOCK = 64
ATTN_HEAD_GROUP = 4
MASK_BIAS = -(2.0 ** 100)
Q_SCALE = DSA_HEAD_DIM ** -0.5 * math.log2(math.e)

COL_QKV_A = 0
COL_Z_A = 3 * GDN_WIDTH
COL_QKV_B = COL_Z_A + GDN_WIDTH
COL_G_A = COL_QKV_B + 3 * DSA_WIDTH
COL_Q_I = COL_G_A + 2 * 1024
MAIN_COLS = COL_Q_I + IDX_HEADS * IDX_HEAD_DIM
SM_A, SM_B, SM_W, SM_K = 0, 8, 16, 64


def _cparams(*sem):
    return pltpu.CompilerParams(dimension_semantics=sem, vmem_limit_bytes=VMEM_LIMIT)


def _rmsnorm_rows(x, gain):
    ms = jnp.mean(x * x, axis=-1, keepdims=True)
    return x * lax.rsqrt(ms + NORM_EPS) * gain


def _dot(a, b):
    return jnp.dot(a, b, preferred_element_type=F32)


def _dot_nt(a, b):
    return lax.dot_general(a, b, (((1,), (1,)), ((), ())), preferred_element_type=F32)


def _resident(shape):
    return pl.BlockSpec(shape, lambda i, j: (0,) * len(shape), pipeline_mode=pl.Buffered(1))


def _norm_mm_body(x_ref, g_ref, w_ref, o_ref, xn_ref, *, tn, exact):
    @pl.when(pl.program_id(1) == 0)
    def _():
        xn_ref[...] = _rmsnorm_rows(x_ref[...], g_ref[...]).astype(xn_ref.dtype)

    w = w_ref[:, pl.ds(pl.multiple_of(pl.program_id(1) * tn, tn), tn)]
    if exact:
        x = xn_ref[...]
        x_hi, w_hi = x.astype(BF16), w.astype(BF16)
        x_lo = (x - x_hi.astype(F32)).astype(BF16)
        w_lo = (w - w_hi.astype(F32)).astype(BF16)
        o = _dot(x_hi, w_hi) + (_dot(x_hi, w_lo) + _dot(x_lo, w_hi))
    else:
        o = _dot(xn_ref[...], w)
    o_ref[...] = o.astype(o_ref.dtype)


def _norm_matmul(x, gain, w, *, tm, tn, out_dtype, exact=False):
    m, k = x.shape
    n = w.shape[1]
    return pl.pallas_call(
        functools.partial(_norm_mm_body, tn=tn, exact=exact),
        grid=(m // tm, n // tn),
        in_specs=[pl.BlockSpec((tm, k), lambda i, j: (i, 0)),
                  _resident((1, k)),
                  _resident((k, n))],
        out_specs=pl.BlockSpec((tm, tn), lambda i, j: (i, j)),
        out_shape=jax.ShapeDtypeStruct((m, n), out_dtype),
        scratch_shapes=[pltpu.VMEM((tm, k), w.dtype)],
        compiler_params=_cparams("parallel", "arbitrary"),
        name="norm_matmul_exact" if exact else "norm_matmul",
    )(x, gain.reshape(1, k), w)


def _resident1(shape):
    return pl.BlockSpec(shape, lambda i: (0,) * len(shape), pipeline_mode=pl.Buffered(1))


def _ffn_body(x_ref, g_ref, wgu_ref, wd_ref, *rest, f, final_norm):
    if final_norm:
        gf_ref, o_ref = rest
    else:
        (o_ref,) = rest
    x = x_ref[...]
    xn = _rmsnorm_rows(x, g_ref[...]).astype(BF16)
    g = _dot(xn, wgu_ref[:, :f])
    u = _dot(xn, wgu_ref[:, f:])
    act = (g * jax.nn.sigmoid(g) * u).astype(BF16)
    h = x + _dot(act, wd_ref[...])
    if final_norm:
        h = _rmsnorm_rows(h, gf_ref[...])
    o_ref[...] = h


def _ffn(x, gain, w_gate_up, w_down, *, tm, final_gain=None):
    m, k = x.shape
    f = w_down.shape[0]
    final_norm = final_gain is not None
    in_specs = [pl.BlockSpec((tm, k), lambda i: (i, 0)),
                _resident1((1, k)), _resident1((k, 2 * f)), _resident1((f, k))]
    args = [x, gain.reshape(1, k), w_gate_up, w_down]
    if final_norm:
        in_specs.append(_resident1((1, k)))
        args.append(final_gain.reshape(1, k))
    return pl.pallas_call(
        functools.partial(_ffn_body, f=f, final_norm=final_norm),
        grid=(m // tm,),
        in_specs=in_specs,
        out_specs=pl.BlockSpec((tm, k), lambda i: (i, 0)),
        out_shape=jax.ShapeDtypeStruct((m, k), F32),
        compiler_params=_cparams("parallel"),
        name="ffn_norm" if final_norm else "ffn",
    )(*args)


def _mixer_out_body(oa_ref, ob_ref, wa_ref, wb_ref, wo_ref, ga_ref, gb_ref, h_ref, o_ref):
    ya = _dot(oa_ref[...], wa_ref[...])
    yb = _dot(ob_ref[...], wb_ref[...])
    ga = jax.nn.sigmoid(ga_ref[...].astype(F32))
    gb = jax.nn.sigmoid(gb_ref[...].astype(F32))
    merged = (ga * ya + gb * yb).astype(BF16)
    o_ref[...] = h_ref[...] + _dot(merged, wo_ref[...])


def _mixer_out(oa, ob, wa, wb, wo, proj, h, *, tm):
    m, k = oa.shape
    n = wa.shape[1]
    row = lambda w: pl.BlockSpec((tm, w), lambda i: (i, 0))
    return pl.pallas_call(
        _mixer_out_body,
        grid=(m // tm,),
        in_specs=[row(k), row(k), _resident1((k, n)), _resident1((k, n)), _resident1((n, n)),
                  pl.BlockSpec((tm, n), lambda i: (i, COL_G_A // n)),
                  pl.BlockSpec((tm, n), lambda i: (i, COL_G_A // n + 1)),
                  row(n)],
        out_specs=row(n),
        out_shape=jax.ShapeDtypeStruct((m, n), F32),
        compiler_params=_cparams("parallel"),
        name="mixer_out",
    )(oa, ob, wa, wb, wo, proj, proj, h)


HALO = 16
CONV_ROWS = 128


def _gdn_prep_body(x_ref, halo_ref, cw_ref, q_ref, k_ref, v_ref, xs_ref, *, tl):
    i = pl.program_id(1)
    sub = CONV_ROWS
    xs_ref[0:HALO, :] = jnp.where(i == 0, jnp.zeros_like(halo_ref[0]), halo_ref[0])
    xs_ref[HALO:HALO + tl, :] = x_ref[0]
    t = lax.broadcasted_iota(jnp.int32, ((CONV_K - 1) * sub, HALO + sub), 0)
    c = lax.broadcasted_iota(jnp.int32, ((CONV_K - 1) * sub, HALO + sub), 1)
    shift = jnp.where(c == t % sub + HALO - (CONV_K - 1) + t // sub, 1.0, 0.0).astype(BF16)
    for r0 in range(0, tl, sub):
        win = xs_ref[r0:r0 + HALO + sub, :]
        z = _dot(shift, win)
        y = cw_ref[CONV_K - 1:CONV_K, :] * win[HALO:, :].astype(F32)
        for j in range(CONV_K - 1):
            y = y + cw_ref[j:j + 1, :] * z[j * sub:(j + 1) * sub]
        y = y * jax.nn.sigmoid(y)
        rows = slice(r0, r0 + sub)
        for h in range(GDN_HEADS):
            lo = h * GDN_HEAD_DIM
            qh = y[:, lo:lo + GDN_HEAD_DIM]
            kh = y[:, GDN_WIDTH + lo:GDN_WIDTH + lo + GDN_HEAD_DIM]
            qn = qh * lax.rsqrt(jnp.sum(qh * qh, axis=-1, keepdims=True) + NORM_EPS)
            kn = kh * lax.rsqrt(jnp.sum(kh * kh, axis=-1, keepdims=True) + NORM_EPS)
            q_ref[0, rows, lo:lo + GDN_HEAD_DIM] = (qn * (GDN_HEAD_DIM ** -0.5)).astype(q_ref.dtype)
            k_ref[0, rows, lo:lo + GDN_HEAD_DIM] = kn.astype(k_ref.dtype)
        v_ref[0, rows, :] = y[:, 2 * GDN_WIDTH:].astype(v_ref.dtype)


def _gdn_prep(proj, conv_w, *, tl):
    b, l, _ = proj.shape
    c = 3 * GDN_WIDTH
    r = tl // HALO
    out = jax.ShapeDtypeStruct((b, l, GDN_WIDTH), BF16)
    ospec = pl.BlockSpec((1, tl, GDN_WIDTH), lambda bi, i: (bi, i, 0))
    return pl.pallas_call(
        functools.partial(_gdn_prep_body, tl=tl),
        grid=(b, l // tl),
        in_specs=[pl.BlockSpec((1, tl, c), lambda bi, i: (bi, i, 0)),
                  pl.BlockSpec((1, HALO, c), lambda bi, i: (bi, jnp.maximum(i * r - 1, 0), 0)),
                  pl.BlockSpec((CONV_K, c), lambda bi, i: (0, 0))],
        out_specs=[ospec, ospec, ospec],
        out_shape=[out, out, out],
        scratch_shapes=[pltpu.VMEM((HALO + tl, c), BF16)],
        compiler_params=_cparams("parallel", "parallel"),
        name="gdn_prep",
    )(proj, proj, conv_w)


def _gdn_chunk_body(q_ref, k_ref, v_ref, z_ref, sm_ref, alog_ref, dtb_ref, gn_ref, o_ref,
                    s_ref, wq_ref, qkk_ref, u_ref, gl_ref, *, nch, nb):
    c = GDN_CHUNK
    d = GDN_HEAD_DIM
    probs = [(bb, h) for bb in range(nb) for h in range(GDN_HEADS)]

    @pl.when(pl.program_id(1) == 0)
    def _():
        s_ref[...] = jnp.zeros_like(s_ref)

    row = lax.broadcasted_iota(jnp.int32, (c, c), 0)
    col = lax.broadcasted_iota(jnp.int32, (c, c), 1)
    tril_incl = row >= col
    tril_strict = row > col
    eye = (row == col).astype(F32)
    row_id = lax.broadcasted_iota(jnp.int32, (c, LANES), 0)

    def intra(ci, carry):
        r0 = pl.multiple_of(ci * c, c)
        gam_col, gam_row, beta_all = [], [], []
        for bb in range(nb):
            sm = sm_ref[bb, pl.ds(r0, c), :]
            g_all = -jnp.exp(alog_ref[...]) * jax.nn.softplus(sm + dtb_ref[...])
            beta_all.append(jax.nn.sigmoid(sm))
            gam = g_all
            step = 1
            while step < c:
                gam = gam + jnp.where(row_id >= step, pltpu.roll(gam, step, axis=0), 0.0)
                step *= 2
            gam_col.append(gam)
            gam_row.append(gam.T)
        gc = [gam_col[bb][:, SM_A + h:SM_A + h + 1] for bb, h in probs]
        bc = [beta_all[bb][:, SM_B + h:SM_B + h + 1] for bb, h in probs]
        g_last = [g[c - 1:c, :] for g in gc]
        decay = [jnp.exp(jnp.minimum(g - gam_row[bb][SM_A + h:SM_A + h + 1, :], 0.0))
                 for g, (bb, h) in zip(gc, probs)]
        qh = [q_ref[bb, pl.ds(r0, c), h * d:(h + 1) * d] for bb, h in probs]
        kh = [k_ref[bb, pl.ds(r0, c), h * d:(h + 1) * d] for bb, h in probs]
        vh = [v_ref[bb, pl.ds(r0, c), h * d:(h + 1) * d] for bb, h in probs]
        kq = [_dot_nt(jnp.concatenate([k, q], axis=0), k) for k, q in zip(kh, qh)]
        qk = [jnp.where(tril_incl, x[c:] * dc, 0.0) for x, dc in zip(kq, decay)]
        n_pow = [-jnp.where(tril_strict, b_ * x[:c] * dc, 0.0) for x, b_, dc in zip(kq, bc, decay)]
        t_inv = [eye + n for n in n_pow]
        n_pow = [_dot(n.astype(BF16), n.astype(BF16)) for n in n_pow]
        for _ in range(int(math.log2(c)) - 2):
            prod = [_dot(jnp.concatenate([n, t], axis=0).astype(BF16), n.astype(BF16))
                    for n, t in zip(n_pow, t_inv)]
            n_pow = [p[:c] for p in prod]
            t_inv = [t + p[c:] for t, p in zip(t_inv, prod)]
        t_inv = [t + _dot(t.astype(BF16), n.astype(BF16)) for n, t in zip(n_pow, t_inv)]
        e_gc = [jnp.exp(g) for g in gc]
        kf = [k.astype(F32) for k in kh]
        rhs = [jnp.concatenate([v.astype(F32) * b_, k * (b_ * e)], axis=1)
               for v, k, b_, e in zip(vh, kf, bc, e_gc)]
        uw = [_dot(t.astype(BF16), r.astype(BF16)) for t, r in zip(t_inv, rhs)]
        for i, (bb, h) in enumerate(probs):
            k_dec = kf[i] * jnp.exp(g_last[i] - gc[i])
            u_ref[bb, ci, h] = uw[i][:, :d]
            wq_ref[bb, ci, h] = jnp.concatenate([uw[i][:, d:], qh[i].astype(F32) * e_gc[i]], axis=0).astype(BF16)
            qkk_ref[bb, ci, h] = jnp.concatenate([qk[i], k_dec.T], axis=0).astype(BF16)
            gl_ref[bb, ci, h] = jnp.broadcast_to(jnp.exp(g_last[i]), (1, d))
        return carry

    lax.fori_loop(0, nch, intra, 0)

    def scan(ci, carry):
        r0 = pl.multiple_of(ci * c, c)
        s_old = [s_ref[bb, h] for bb, h in probs]
        m1 = [_dot(wq_ref[bb, ci, h], s.astype(BF16)) for s, (bb, h) in zip(s_old, probs)]
        v_new = [u_ref[bb, ci, h] - m[:c] for m, (bb, h) in zip(m1, probs)]
        m2 = [_dot(qkk_ref[bb, ci, h], v.astype(BF16)) for v, (bb, h) in zip(v_new, probs)]
        for i, (bb, h) in enumerate(probs):
            s_ref[bb, h] = s_old[i] * gl_ref[bb, ci, h] + m2[i][c:]
            o = m1[i][c:] + m2[i][:c]
            o = o * lax.rsqrt(jnp.mean(o * o, axis=-1, keepdims=True) + NORM_EPS) * gn_ref[...]
            z = z_ref[bb, pl.ds(r0, c), h * d:(h + 1) * d].astype(F32)
            o_ref[bb, pl.ds(r0, c), h * d:(h + 1) * d] = (o * (z * jax.nn.sigmoid(z))).astype(o_ref.dtype)
        return carry

    lax.fori_loop(0, nch, scan, 0)


def _gdn_chunk(q, k, v, proj, smalls, a_log, dt_bias, gdn_norm, *, tl):
    b, l, _ = q.shape
    c, d, nh = GDN_CHUNK, GDN_HEAD_DIM, GDN_HEADS
    nch = tl // c
    nb = 2 if b % 2 == 0 else 1
    pad = lambda t, at: jnp.zeros((1, LANES), F32).at[0, at:at + t.shape[0]].set(t.astype(F32))
    qspec = pl.BlockSpec((nb, tl, GDN_WIDTH), lambda bi, i: (bi, i, 0))
    vec = pl.BlockSpec((1, LANES), lambda bi, i: (0, 0))
    return pl.pallas_call(
        functools.partial(_gdn_chunk_body, nch=nch, nb=nb),
        grid=(b // nb, l // tl),
        in_specs=[qspec, qspec, qspec,
                  pl.BlockSpec((nb, tl, GDN_WIDTH), lambda bi, i: (bi, i, COL_Z_A // GDN_WIDTH)),
                  pl.BlockSpec((nb, tl, LANES), lambda bi, i: (bi, i, 0)),
                  vec, vec, vec],
        out_specs=qspec,
        out_shape=jax.ShapeDtypeStruct((b, l, GDN_WIDTH), BF16),
        scratch_shapes=[pltpu.VMEM((nb, nh, d, d), F32),
                        pltpu.VMEM((nb, nch, nh, 2 * c, d), BF16),
                        pltpu.VMEM((nb, nch, nh, c + d, c), BF16),
                        pltpu.VMEM((nb, nch, nh, c, d), F32),
                        pltpu.VMEM((nb, nch, nh, 1, d), F32)],
        compiler_params=_cparams("parallel", "arbitrary"),
        name="gdn_chunk",
    )(q, k, v, proj, smalls, pad(a_log, SM_A), pad(dt_bias, SM_A), gdn_norm.reshape(1, LANES).astype(F32))


def _swap_halves(x, half):
    if 2 * half == LANES:
        return pltpu.roll(x, half, axis=1)
    lane = lax.broadcasted_iota(jnp.int32, x.shape, 1)
    first = (lane % (2 * half)) < half
    return jnp.where(first, pltpu.roll(x, LANES - half, axis=1), pltpu.roll(x, half, axis=1))


def _dsa_prep_body(qk_ref, v_ref, qi_ref, sm_ref, ca_ref, sa_ref, ci_ref, si_ref,
                   qb_ref, kb_ref, vx_ref, qio_ref, kio_ref):
    ca, sa = ca_ref[...], sa_ref[...]
    ones = jnp.ones((v_ref.shape[1], DSA_HEAD_DIM), BF16)
    for h in range(DSA_HEADS):
        lo = h * DSA_HEAD_DIM
        vx_ref[0, :, 2 * lo:2 * lo + DSA_HEAD_DIM] = v_ref[0, :, lo:lo + DSA_HEAD_DIM]
        vx_ref[0, :, 2 * lo + DSA_HEAD_DIM:2 * lo + 2 * DSA_HEAD_DIM] = ones
    for h in range(2 * DSA_HEADS):
        lo = h * DSA_HEAD_DIM
        x = qk_ref[0, :, lo:lo + DSA_HEAD_DIM].astype(F32)
        y = x * ca + _swap_halves(x, DSA_HEAD_DIM // 2) * sa
        if h < DSA_HEADS:
            qb_ref[0, :, lo:lo + DSA_HEAD_DIM] = (y * Q_SCALE).astype(BF16)
        else:
            lo -= DSA_WIDTH
            kb_ref[0, :, lo:lo + DSA_HEAD_DIM] = y.astype(BF16)
    ci, si = ci_ref[...], si_ref[...]
    for p in range(IDX_HEADS * IDX_HEAD_DIM // LANES):
        lo = p * LANES
        x = qi_ref[0, :, lo:lo + LANES].astype(F32)
        y = x * ci + _swap_halves(x, IDX_HEAD_DIM // 2) * si
        qio_ref[0, :, lo:lo + LANES] = y.astype(BF16)
    x = sm_ref[0]
    y = x * ci + _swap_halves(x, IDX_HEAD_DIM // 2) * si
    kio_ref[0] = (y[:, SM_K:] * (IDX_HEAD_DIM ** -0.5)).astype(BF16)


def _dsa_prep(proj, smalls, tabs, *, tl):
    b, l, _ = proj.shape
    w2 = 2 * DSA_WIDTH
    wi = IDX_HEADS * IDX_HEAD_DIM
    tab = pl.BlockSpec((tl, LANES), lambda bi, i: (i, 0))
    big = lambda n: pl.BlockSpec((1, tl, n), lambda bi, i: (bi, i, 0))
    return pl.pallas_call(
        _dsa_prep_body,
        grid=(b, l // tl),
        in_specs=[pl.BlockSpec((1, tl, w2), lambda bi, i: (bi, i, COL_QKV_B // w2)),
                  pl.BlockSpec((1, tl, DSA_WIDTH), lambda bi, i: (bi, i, (COL_QKV_B + w2) // DSA_WIDTH)),
                  pl.BlockSpec((1, tl, wi), lambda bi, i: (bi, i, COL_Q_I // wi)),
                  big(LANES), tab, tab, tab, tab],
        out_specs=[big(DSA_WIDTH), big(DSA_WIDTH), big(w2), big(wi), big(IDX_HEAD_DIM)],
        out_shape=[jax.ShapeDtypeStruct((b, l, DSA_WIDTH), BF16),
                   jax.ShapeDtypeStruct((b, l, DSA_WIDTH), BF16),
                   jax.ShapeDtypeStruct((b, l, w2), BF16),
                   jax.ShapeDtypeStruct((b, l, wi), BF16),
                   jax.ShapeDtypeStruct((b, l, IDX_HEAD_DIM), BF16)],
        compiler_params=_cparams("parallel", "parallel"),
        name="dsa_prep",
    )(proj, proj, proj, smalls, *tabs)


def _dsa_select_body(qi_ref, sm_ref, ki_ref, bias_in_ref, o_ref, s_ref, *, tq, tc, topk, n_c, tile0):
    del bias_in_ref
    nl = tc // LANES
    half = tq // 2
    q0 = (tile0 + pl.program_id(1)) * tq
    kf = float(topk)
    w_all = sm_ref[0] * (IDX_HEADS ** -0.5)
    rb = min(SELECT_ROW_BLOCK, half)
    rep = lambda col: jnp.broadcast_to(col, (col.shape[0], LANES))
    lane_tiles = lambda x: [x[:, j * LANES:(j + 1) * LANES] for j in range(nl)]

    def score_chunk(k0, carry, last):
        mn, mx, n_pos, n_nonneg = carry
        kc = ki_ref[0, pl.ds(k0, tc), :]
        s = jnp.zeros((tq, tc), F32)
        for h in range(IDX_HEADS):
            qh = qi_ref[0, :, h * IDX_HEAD_DIM:(h + 1) * IDX_HEAD_DIM]
            s = s + w_all[:, SM_W + h:SM_W + h + 1] * jnp.maximum(_dot_nt(qh, kc), 0.0)
        if last:
            qpos = q0 + lax.broadcasted_iota(jnp.int32, (tq, tc), 0)
            causal = k0 + lax.broadcasted_iota(jnp.int32, (tq, tc), 1) <= qpos
            s_lo, s_hi = jnp.where(causal, s, -jnp.inf), jnp.where(causal, s, jnp.inf)
        else:
            s_lo = s_hi = s
        s_ref[:, pl.ds(k0, tc)] = s_lo
        for lo_t, hi_t in zip(lane_tiles(s_lo), lane_tiles(s_hi)):
            mx = jnp.maximum(mx, lo_t)
            mn = jnp.minimum(mn, hi_t)
            n_pos = n_pos + jnp.where(lo_t > 0.0, 1.0, 0.0)
            n_nonneg = n_nonneg + jnp.where(lo_t >= 0.0, 1.0, 0.0)
        return mn, mx, n_pos, n_nonneg

    zeros = jnp.zeros((tq, LANES), F32)
    stats = lax.fori_loop(0, n_c - 1, lambda ci, c: score_chunk(pl.multiple_of(ci * tc, tc), c, False),
                          (zeros + jnp.inf, zeros - jnp.inf, zeros, zeros), unroll=2)
    mn, mx, n_pos, n_nonneg = score_chunk((n_c - 1) * tc, stats, True)
    row_min = rep(jnp.min(mn, axis=1, keepdims=True))
    row_max = rep(jnp.max(mx, axis=1, keepdims=True))
    n_pos = rep(jnp.sum(n_pos, axis=1, keepdims=True))
    n_nonneg = rep(jnp.sum(n_nonneg, axis=1, keepdims=True))

    def reduce_keys(fn, init, combine, r_lo, r_hi, unrolled):
        accs = []
        for r0 in range(r_lo, r_hi, rb):
            def chunk(k0, acc, r0=r0):
                for j in range(nl):
                    acc = combine(acc, fn(s_ref[r0:r0 + rb, pl.ds(k0 + j * LANES, LANES)], r0 - r_lo))
                return acc
            acc = jnp.full((rb, LANES), init, F32)
            if unrolled:
                for ci in range(n_c):
                    acc = chunk(ci * tc, acc)
            else:
                acc = lax.fori_loop(0, n_c, lambda ci, a: chunk(pl.multiple_of(ci * tc, tc), a), acc)
            accs.append(acc)
        return jnp.concatenate(accs, axis=0)

    def count_acc(op, th, r_lo, r_hi, unrolled=True):
        return reduce_keys(lambda t, r: jnp.where(op(t, th[r:r + rb]), 1.0, 0.0), 0.0, jnp.add, r_lo, r_hi, unrolled)

    lane_max = lambda x: rep(jnp.max(x, axis=1, keepdims=True))
    lane_sum = lambda x: rep(jnp.sum(x, axis=1, keepdims=True))
    ge = lambda a, b: a >= b
    count = lambda op, th: lane_sum(count_acc(op, th, 0, tq, unrolled=False))
    is_open = lambda lo, hi: jnp.max(hi - lo) > 0.0
    middle = lambda lo, hi: lo + (hi - lo) * 0.5

    def narrow(lo, hi, mid, cnt):
        hit = cnt == kf
        above = cnt > kf
        return jnp.where(hit | above, mid, lo), jnp.where(hit | jnp.logical_not(above), mid, hi)

    n_causal = (q0 + 1 + lax.broadcasted_iota(jnp.int32, (tq, LANES), 0)).astype(F32)
    few = n_causal <= kf
    zero_tied = (n_pos < kf) & (n_nonneg >= kf)
    starts_closed = few | zero_tied
    closed_at = jnp.where(few, -jnp.finfo(F32).max, 0.0)
    positive = n_pos >= kf
    lo = jnp.where(starts_closed, closed_at, jnp.where(positive, jnp.maximum(row_min, 0.0), row_min))
    hi = jnp.where(starts_closed, closed_at, jnp.where(positive, row_max, jnp.minimum(row_max, 0.0)))
    surplus = jnp.where((zero_tied & (n_nonneg > kf)) | (jnp.logical_not(starts_closed) & (lo >= hi)), 1.0, 0.0)

    def search_cond(c):
        return (c[0] < BISECT_STEPS) & (c[7] > 0)

    def search_body(c):
        it, lo_a, hi_a, lo_b, hi_b, acc_b, mid_b, _ = c
        go_on = (is_open(lo_a, hi_a) | is_open(lo_b, hi_b)).astype(jnp.int32)
        mid_a = middle(lo_a, hi_a)
        acc_a = count_acc(ge, mid_a, 0, half)
        lo_b, hi_b = narrow(lo_b, hi_b, mid_b, lane_sum(acc_b))
        mid_b = middle(lo_b, hi_b)
        acc_b = count_acc(ge, mid_b, half, tq)
        lo_a, hi_a = narrow(lo_a, hi_a, mid_a, lane_sum(acc_a))
        return it + 1, lo_a, hi_a, lo_b, hi_b, acc_b, mid_b, go_on

    lo_a, hi_a, lo_b, hi_b = lo[:half], hi[:half], lo[half:], hi[half:]
    mid_b = middle(lo_b, hi_b)
    _, lo_a, hi_a, lo_b, hi_b, acc_b, mid_b, _ = lax.while_loop(
        search_cond, search_body,
        (jnp.int32(0), lo_a, hi_a, lo_b, hi_b, count_acc(ge, mid_b, half, tq), mid_b, jnp.int32(1)))
    lo_b, hi_b = narrow(lo_b, hi_b, mid_b, lane_sum(acc_b))
    lo = jnp.concatenate([lo_a, lo_b], axis=0)
    hi = jnp.concatenate([hi_a, hi_b], axis=0)

    def peel_cond(c):
        return (c[0] < topk + 2) & (c[4] > 0)

    def peel_body(c):
        it, lo, hi, surplus, _ = c
        still = hi > lo
        widen = jnp.where(it == 0, jnp.finfo(F32).max, 0.0)
        below = jnp.where(hi == row_max, hi + widen, hi)
        top = lane_max(reduce_keys(
            lambda t, r: jnp.where((t >= lo[r:r + rb]) & (t < below[r:r + rb]), t, -jnp.inf),
            -jnp.inf, jnp.maximum, 0, tq, False))
        n_top = count(ge, top)
        found = still & (n_top >= kf)
        lo = jnp.where(found, top, lo)
        hi = jnp.where(still, top, hi)
        surplus = jnp.where(found & (n_top > kf), 1.0, surplus)
        return it + 1, lo, hi, surplus, is_open(lo, hi).astype(jnp.int32)

    _, thr, _, surplus, _ = lax.while_loop(
        peel_cond, peel_body, (jnp.int32(0), lo, hi, surplus, is_open(lo, hi).astype(jnp.int32)))

    for r0 in range(0, tq, rb):
        rows = slice(r0, r0 + rb)
        thr_r = thr[rows]
        has_excess = jnp.max(surplus[rows]) > 0.0

        def write_plain(ci, carry, rows=rows, thr_r=thr_r):
            k0 = pl.multiple_of(ci * tc, tc)
            for j in range(nl):
                cols = pl.ds(k0 + j * LANES, LANES)
                o_ref[0, rows, cols] = jnp.where(s_ref[rows, cols] >= thr_r, 0.0, MASK_BIAS).astype(o_ref.dtype)
            return carry

        @pl.when(jnp.logical_not(has_excess))
        def _(write_plain=write_plain):
            lax.fori_loop(0, n_c, write_plain, 0)

        @pl.when(has_excess)
        def _(r0=r0, rows=rows, thr_r=thr_r):
            n_gt = lane_sum(count_acc(lambda a, b: a > b, thr_r, r0, r0 + rb, unrolled=False))
            need_c = (kf - n_gt)[:, 0:1]
            thr_c = thr_r[:, 0:1]
            r = lax.broadcasted_iota(jnp.int32, (tc, tc), 0)
            cidx = lax.broadcasted_iota(jnp.int32, (tc, tc), 1)
            upper = jnp.where(r < cidx, 1.0, 0.0).astype(BF16)

            def write_ties(ci, seen):
                k0 = pl.multiple_of(ci * tc, tc)
                sc = s_ref[rows, pl.ds(k0, tc)]
                eq = sc == thr_c
                before = seen + _dot(jnp.where(eq, 1.0, 0.0).astype(BF16), upper)
                keep = (sc > thr_c) | (eq & (before < need_c))
                o_ref[0, rows, pl.ds(k0, tc)] = jnp.where(keep, 0.0, MASK_BIAS).astype(o_ref.dtype)
                return seen + jnp.sum(jnp.where(eq, 1.0, 0.0), axis=1, keepdims=True)

            lax.fori_loop(0, n_c, write_ties, jnp.zeros((rb, 1), F32))


def _dsa_select(qi, smalls, ki, bias, *, tq, tc, topk):
    b, l, wi = qi.shape
    per_call = tc // tq
    for n_c in range(1, l // tc + 1):
        tile0 = (n_c - 1) * per_call
        bias = pl.pallas_call(
            functools.partial(_dsa_select_body, tq=tq, tc=tc, topk=topk, n_c=n_c, tile0=tile0),
            grid=(b, per_call),
            in_specs=[pl.BlockSpec((1, tq, wi), lambda bi, t, tile0=tile0: (bi, tile0 + t, 0)),
                      pl.BlockSpec((1, tq, LANES), lambda bi, t, tile0=tile0: (bi, tile0 + t, 0)),
                      pl.BlockSpec((1, n_c * tc, IDX_HEAD_DIM), lambda bi, t: (bi, 0, 0)),
                      pl.BlockSpec(memory_space=pl.ANY)],
            out_specs=pl.BlockSpec((1, tq, n_c * tc), lambda bi, t, tile0=tile0: (bi, tile0 + t, 0)),
            out_shape=jax.ShapeDtypeStruct((b, l, l), BF16),
            scratch_shapes=[pltpu.VMEM((tq, n_c * tc), F32)],
            input_output_aliases={3: 0},
            compiler_params=_cparams("parallel", "parallel"),
            name=f"dsa_select_{n_c}",
        )(qi, smalls, ki, bias)
    return bias


def _dsa_attn_body(qt_ref, kt_ref, q_ref, k_ref, v_ref, b_ref, o_ref, m_ref, acc_ref):
    step = pl.program_id(1)
    qi, ki = qt_ref[step], kt_ref[step]
    d = DSA_HEAD_DIM

    @pl.when(ki == 0)
    def _():
        m_ref[...] = jnp.full_like(m_ref, MASK_BIAS)
        acc_ref[...] = jnp.zeros_like(acc_ref)

    bias = b_ref[0]
    for h0 in range(0, DSA_HEADS, ATTN_HEAD_GROUP):
        heads = range(h0, h0 + ATTN_HEAD_GROUP)
        s = [_dot_nt(q_ref[0, :, h * d:(h + 1) * d], k_ref[0, :, h * d:(h + 1) * d]).astype(BF16) + bias
             for h in heads]
        m_old = [m_ref[h] for h in heads]
        m_new = [jnp.maximum(mo, jnp.max(x, axis=-1, keepdims=True).astype(F32)) for mo, x in zip(m_old, s)]
        alpha = [jnp.exp2(mo - mn) for mo, mn in zip(m_old, m_new)]
        p = [jnp.exp2(x - mn[:, 0:1].astype(BF16)) for x, mn in zip(s, m_new)]
        upd = [_dot(x, v_ref[0, :, 2 * h * d:2 * (h + 1) * d]) for x, h in zip(p, heads)]
        for i, h in enumerate(heads):
            acc_ref[h, :, :d] = alpha[i] * acc_ref[h, :, :d] + upd[i][:, :d]
            acc_ref[h, :, d:] = alpha[i] * acc_ref[h, :, d:] + upd[i][:, d:]
            m_ref[h] = m_new[i]

    @pl.when(ki == qi)
    def _():
        for h in range(DSA_HEADS):
            lo = h * d
            acc = acc_ref[h]
            o_ref[0, :, lo:lo + d] = (acc[:, :d] / acc[:, d:]).astype(o_ref.dtype)


def _dsa_attention(qb, kb, vx, bias, *, t):
    b, l, w = qb.shape
    n = l // t
    pairs = [(i, j) for i in range(n) for j in range(i + 1)]
    q_tile = jnp.asarray([p[0] for p in pairs], jnp.int32)
    k_tile = jnp.asarray([p[1] for p in pairs], jnp.int32)
    q_idx = lambda bi, s, qt, kt: (bi, qt[s], 0)
    kv_idx = lambda bi, s, qt, kt: (bi, kt[s], 0)
    return pl.pallas_call(
        _dsa_attn_body,
        grid_spec=pltpu.PrefetchScalarGridSpec(
            num_scalar_prefetch=2,
            grid=(b, len(pairs)),
            in_specs=[pl.BlockSpec((1, t, w), q_idx),
                      pl.BlockSpec((1, t, w), kv_idx),
                      pl.BlockSpec((1, t, 2 * w), kv_idx),
                      pl.BlockSpec((1, t, t), lambda bi, s, qt, kt: (bi, qt[s], kt[s]))],
            out_specs=pl.BlockSpec((1, t, w), q_idx),
            scratch_shapes=[pltpu.VMEM((DSA_HEADS, t, LANES), F32),
                            pltpu.VMEM((DSA_HEADS, t, 2 * DSA_HEAD_DIM), F32)]),
        out_shape=jax.ShapeDtypeStruct((b, l, w), BF16),
        compiler_params=_cparams("parallel", "arbitrary"),
        name="dsa_attention",
    )(q_tile, k_tile, qb, kb, vx, bias)


def _rope_tables(seq_len):
    def tab(dim):
        inv_freq = 1.0 / (ROPE_THETA ** (jnp.arange(0, dim, 2, dtype=F32) / dim))
        split = math.gcd(seq_len, ROPE_SPLIT)
        ang_lo = jnp.arange(split, dtype=F32)[None, :, None] * inv_freq
        ang_hi = (jnp.arange(seq_len // split, dtype=F32) * split)[:, None, None] * inv_freq
        c = (jnp.cos(ang_hi) * jnp.cos(ang_lo) - jnp.sin(ang_hi) * jnp.sin(ang_lo)).reshape(seq_len, dim // 2)
        s = (jnp.sin(ang_hi) * jnp.cos(ang_lo) + jnp.cos(ang_hi) * jnp.sin(ang_lo)).reshape(seq_len, dim // 2)
        return jnp.concatenate([c, c], axis=1), jnp.concatenate([-s, s], axis=1)
    ca, sa = tab(DSA_HEAD_DIM)
    c64, s64 = tab(IDX_HEAD_DIM)
    ci = jnp.concatenate([c64, c64], axis=1)
    si = jnp.concatenate([s64, s64], axis=1)
    return ca, sa, ci, si


def _regroup_w_in(w):
    sizes = (3 * GDN_WIDTH, GDN_WIDTH, GDN_HEADS, GDN_HEADS, 3 * DSA_WIDTH,
             IDX_HEADS * IDX_HEAD_DIM, IDX_HEAD_DIM, IDX_HEADS, 1024, 1024)
    parts, start = [], 0
    for s in sizes:
        parts.append(w[:, start:start + s])
        start += s
    qkv_a, z_a, a_a, b_a, qkv_b, q_i, k_i, w_i, g_a, g_b = parts
    main = jnp.concatenate([qkv_a, z_a, qkv_b, g_a, g_b, q_i], axis=1).astype(BF16)
    pad = jnp.zeros((w.shape[0], SM_K - SM_W - IDX_HEADS), w.dtype)
    smalls = jnp.concatenate([a_a, b_a, w_i, pad, k_i], axis=1)
    return main, smalls


def kernel(x, norm_mix, w_in, conv_w, a_log, dt_bias, gdn_norm, w_out_gdn, w_out_dsa, w_o,
           norm_ffn, w_gate_up, w_down, norm_final):
    b, l, dm = x.shape
    depth = w_in.shape[0]
    m = b * l
    topk = min(TOPK_MAX, l // 4)
    tabs = _rope_tables(l)
    tm = min(1024, m)
    t_attn = min(512, l)
    tq_sel = min(256, l)
    h = x.reshape(m, dm)
    bias = jnp.full((b, l, l), MASK_BIAS, BF16)
    for layer in range(depth):
        w_main, w_smalls = _regroup_w_in(w_in[layer])
        proj = _norm_matmul(h, norm_mix[layer], w_main, tm=min(512, m), tn=MAIN_COLS // 4, out_dtype=BF16)
        smalls = _norm_matmul(h, norm_mix[layer], w_smalls, tm=tm, tn=LANES, out_dtype=F32, exact=True)
        proj = proj.reshape(b, l, MAIN_COLS)
        smalls = smalls.reshape(b, l, LANES)

        q_a, k_a, v_a = _gdn_prep(proj, conv_w[layer], tl=min(256, l))
        o_a = _gdn_chunk(q_a, k_a, v_a, proj, smalls, a_log[layer], dt_bias[layer], gdn_norm[layer],
                         tl=min(512, l))

        q_b, k_b, v_x, q_i, k_i = _dsa_prep(proj, smalls, tabs, tl=min(512, l))
        bias = _dsa_select(q_i, smalls, k_i, bias, tq=tq_sel, tc=min(512, l), topk=topk)
        o_b = _dsa_attention(q_b, k_b, v_x, bias, t=t_attn)

        h = _mixer_out(o_a.reshape(m, GDN_WIDTH), o_b.reshape(m, DSA_WIDTH),
                       w_out_gdn[layer].astype(BF16), w_out_dsa[layer].astype(BF16), w_o[layer].astype(BF16),
                       proj.reshape(m, MAIN_COLS), h, tm=min(512, m))
        last = layer == depth - 1
        h = _ffn(h, norm_ffn[layer], w_gate_up[layer].astype(BF16), w_down[layer].astype(BF16),
                 tm=min(512, m), final_gain=norm_final if last else None)
    return h.reshape(b, l, dm)
```

```python
import functools
import math

import jax
import jax.numpy as jnp
from jax import lax
from jax.experimental import pallas as pl
from jax.experimental.pallas import tpu as pltpu

F32 = jnp.float32
BF16 = jnp.bfloat16

GDN_HEADS = 8
GDN_HEAD_DIM = 128
GDN_WIDTH = GDN_HEADS * GDN_HEAD_DIM
CONV_K = 4
GDN_CHUNK = 64
DSA_HEADS = 8
DSA_HEAD_DIM = 128
DSA_WIDTH = DSA_HEADS * DSA_HEAD_DIM
IDX_HEADS = 8
IDX_HEAD_DIM = 64
TOPK_MAX = 256
ROPE_THETA = 10000.0
ROPE_SPLIT = 64
NORM_EPS = 1e-6

LANES = 128
VMEM_LIMIT = 48 * 1024 * 1024
BISECT_STEPS = 32
SELECT_ROW_BLOCK = 64
ATTN_HEAD_GROUP = 4
MASK_BIAS = -(2.0 ** 100)
Q_SCALE = DSA_HEAD_DIM ** -0.5 * math.log2(math.e)

COL_QKV_A = 0
COL_Z_A = 3 * GDN_WIDTH
COL_QKV_B = COL_Z_A + GDN_WIDTH
COL_G_A = COL_QKV_B + 3 * DSA_WIDTH
COL_Q_I = COL_G_A + 2 * 1024
MAIN_COLS = COL_Q_I + IDX_HEADS * IDX_HEAD_DIM
SM_A, SM_B, SM_W, SM_K = 0, 8, 16, 64


def _cparams(*sem):
    return pltpu.CompilerParams(dimension_semantics=sem, vmem_limit_bytes=VMEM_LIMIT)


def _rmsnorm_rows(x, gain):
    ms = jnp.mean(x * x, axis=-1, keepdims=True)
    return x * lax.rsqrt(ms + NORM_EPS) * gain


def _dot(a, b):
    return jnp.dot(a, b, preferred_element_type=F32)


def _dot_nt(a, b):
    return lax.dot_general(a, b, (((1,), (1,)), ((), ())), preferred_element_type=F32)


def _resident(shape, layer=None):
    if layer is None:
        return pl.BlockSpec(shape, lambda i, j: (0,) * len(shape), pipeline_mode=pl.Buffered(1))
    return pl.BlockSpec((None,) + shape, lambda i, j: (layer,) + (0,) * len(shape), pipeline_mode=pl.Buffered(1))


def _norm_mm_body(x_ref, g_ref, w_ref, o_ref, xn_ref, *, tn, exact):
    @pl.when(pl.program_id(1) == 0)
    def _():
        xn_ref[...] = _rmsnorm_rows(x_ref[...], g_ref[...]).astype(xn_ref.dtype)

    w = w_ref[:, pl.ds(pl.multiple_of(pl.program_id(1) * tn, tn), tn)]
    if exact:
        x = xn_ref[...]
        x_hi, w_hi = x.astype(BF16), w.astype(BF16)
        x_lo = (x - x_hi.astype(F32)).astype(BF16)
        w_lo = (w - w_hi.astype(F32)).astype(BF16)
        o = _dot(x_hi, w_hi) + (_dot(x_hi, w_lo) + _dot(x_lo, w_hi))
    else:
        o = _dot(xn_ref[...], w)
    o_ref[...] = o.astype(o_ref.dtype)


def _norm_matmul(x, gain, w, layer, *, tm, tn, out_dtype, exact=False):
    m, k = x.shape
    n = w.shape[2]
    return pl.pallas_call(
        functools.partial(_norm_mm_body, tn=tn, exact=exact),
        grid=(m // tm, n // tn),
        in_specs=[pl.BlockSpec((tm, k), lambda i, j: (i, 0)),
                  _resident((1, k)),
                  _resident((k, n), layer)],
        out_specs=pl.BlockSpec((tm, tn), lambda i, j: (i, j)),
        out_shape=jax.ShapeDtypeStruct((m, n), out_dtype),
        scratch_shapes=[pltpu.VMEM((tm, k), w.dtype)],
        compiler_params=_cparams("parallel", "arbitrary"),
        name="norm_matmul_exact" if exact else "norm_matmul",
    )(x, gain.reshape(1, k), w)


def _resident1(shape, layer=None):
    if layer is None:
        return pl.BlockSpec(shape, lambda i: (0,) * len(shape), pipeline_mode=pl.Buffered(1))
    return pl.BlockSpec((None,) + shape, lambda i: (layer,) + (0,) * len(shape), pipeline_mode=pl.Buffered(1))


def _ffn_body(x_ref, g_ref, wgu_ref, wd_ref, *rest, f, final_norm):
    if final_norm:
        gf_ref, o_ref = rest
    else:
        (o_ref,) = rest
    x = x_ref[...]
    xn = _rmsnorm_rows(x, g_ref[...]).astype(BF16)
    g = _dot(xn, wgu_ref[:, :f])
    u = _dot(xn, wgu_ref[:, f:])
    act = (g * jax.nn.sigmoid(g) * u).astype(BF16)
    h = x + _dot(act, wd_ref[...])
    if final_norm:
        h = _rmsnorm_rows(h, gf_ref[...])
    o_ref[...] = h


def _ffn(x, gain, w_gate_up, w_down, layer, *, tm, final_gain=None):
    m, k = x.shape
    f = w_down.shape[1]
    final_norm = final_gain is not None
    in_specs = [pl.BlockSpec((tm, k), lambda i: (i, 0)),
                _resident1((1, k)), _resident1((k, 2 * f), layer), _resident1((f, k), layer)]
    args = [x, gain.reshape(1, k), w_gate_up, w_down]
    if final_norm:
        in_specs.append(_resident1((1, k)))
        args.append(final_gain.reshape(1, k))
    return pl.pallas_call(
        functools.partial(_ffn_body, f=f, final_norm=final_norm),
        grid=(m // tm,),
        in_specs=in_specs,
        out_specs=pl.BlockSpec((tm, k), lambda i: (i, 0)),
        out_shape=jax.ShapeDtypeStruct((m, k), F32),
        compiler_params=_cparams("parallel"),
        name="ffn_norm" if final_norm else "ffn",
    )(*args)


def _mixer_out_body(oa_ref, ob_ref, wa_ref, wb_ref, wo_ref, ga_ref, gb_ref, h_ref, o_ref):
    ya = _dot(oa_ref[...], wa_ref[...])
    yb = _dot(ob_ref[...], wb_ref[...])
    ga = jax.nn.sigmoid(ga_ref[...].astype(F32))
    gb = jax.nn.sigmoid(gb_ref[...].astype(F32))
    merged = (ga * ya + gb * yb).astype(BF16)
    o_ref[...] = h_ref[...] + _dot(merged, wo_ref[...])


def _mixer_out(oa, ob, wa, wb, wo, layer, proj, h, *, tm):
    m, k = oa.shape
    n = wa.shape[2]
    row = lambda w: pl.BlockSpec((tm, w), lambda i: (i, 0))
    return pl.pallas_call(
        _mixer_out_body,
        grid=(m // tm,),
        in_specs=[row(k), row(k), _resident1((k, n), layer), _resident1((k, n), layer), _resident1((n, n), layer),
                  pl.BlockSpec((tm, n), lambda i: (i, COL_G_A // n)),
                  pl.BlockSpec((tm, n), lambda i: (i, COL_G_A // n + 1)),
                  row(n)],
        out_specs=row(n),
        out_shape=jax.ShapeDtypeStruct((m, n), F32),
        compiler_params=_cparams("parallel"),
        name="mixer_out",
    )(oa, ob, wa, wb, wo, proj, proj, h)


HALO = 16
CONV_ROWS = 128


def _gdn_prep_body(x_ref, halo_ref, cw_ref, q_ref, k_ref, v_ref, xs_ref, *, tl):
    i = pl.program_id(1)
    sub = CONV_ROWS
    xs_ref[0:HALO, :] = jnp.where(i == 0, jnp.zeros_like(halo_ref[0]), halo_ref[0])
    xs_ref[HALO:HALO + tl, :] = x_ref[0]
    t = lax.broadcasted_iota(jnp.int32, ((CONV_K - 1) * sub, HALO + sub), 0)
    c = lax.broadcasted_iota(jnp.int32, ((CONV_K - 1) * sub, HALO + sub), 1)
    shift = jnp.where(c == t % sub + HALO - (CONV_K - 1) + t // sub, 1.0, 0.0).astype(BF16)
    for r0 in range(0, tl, sub):
        win = xs_ref[r0:r0 + HALO + sub, :]
        z = _dot(shift, win)
        y = cw_ref[CONV_K - 1:CONV_K, :] * win[HALO:, :].astype(F32)
        for j in range(CONV_K - 1):
            y = y + cw_ref[j:j + 1, :] * z[j * sub:(j + 1) * sub]
        y = y * jax.nn.sigmoid(y)
        rows = slice(r0, r0 + sub)
        for h in range(GDN_HEADS):
            lo = h * GDN_HEAD_DIM
            qh = y[:, lo:lo + GDN_HEAD_DIM]
            kh = y[:, GDN_WIDTH + lo:GDN_WIDTH + lo + GDN_HEAD_DIM]
            qn = qh * lax.rsqrt(jnp.sum(qh * qh, axis=-1, keepdims=True) + NORM_EPS)
            kn = kh * lax.rsqrt(jnp.sum(kh * kh, axis=-1, keepdims=True) + NORM_EPS)
            q_ref[0, rows, lo:lo + GDN_HEAD_DIM] = (qn * (GDN_HEAD_DIM ** -0.5)).astype(q_ref.dtype)
            k_ref[0, rows, lo:lo + GDN_HEAD_DIM] = kn.astype(k_ref.dtype)
        v_ref[0, rows, :] = y[:, 2 * GDN_WIDTH:].astype(v_ref.dtype)


def _gdn_prep(proj, conv_w, *, tl):
    b, l, _ = proj.shape
    c = 3 * GDN_WIDTH
    r = tl // HALO
    out = jax.ShapeDtypeStruct((b, l, GDN_WIDTH), BF16)
    ospec = pl.BlockSpec((1, tl, GDN_WIDTH), lambda bi, i: (bi, i, 0))
    return pl.pallas_call(
        functools.partial(_gdn_prep_body, tl=tl),
        grid=(b, l // tl),
        in_specs=[pl.BlockSpec((1, tl, c), lambda bi, i: (bi, i, 0)),
                  pl.BlockSpec((1, HALO, c), lambda bi, i: (bi, jnp.maximum(i * r - 1, 0), 0)),
                  pl.BlockSpec((CONV_K, c), lambda bi, i: (0, 0))],
        out_specs=[ospec, ospec, ospec],
        out_shape=[out, out, out],
        scratch_shapes=[pltpu.VMEM((HALO + tl, c), BF16)],
        compiler_params=_cparams("parallel", "parallel"),
        name="gdn_prep",
    )(proj, proj, conv_w)


def _gdn_chunk_body(q_ref, k_ref, v_ref, z_ref, sm_ref, alog_ref, dtb_ref, gn_ref, o_ref,
                    s_ref, wq_ref, qkk_ref, u_ref, gl_ref, *, nch, nb):
    c = GDN_CHUNK
    d = GDN_HEAD_DIM
    probs = [(bb, h) for bb in range(nb) for h in range(GDN_HEADS)]

    @pl.when(pl.program_id(1) == 0)
    def _():
        s_ref[...] = jnp.zeros_like(s_ref)

    row = lax.broadcasted_iota(jnp.int32, (c, c), 0)
    col = lax.broadcasted_iota(jnp.int32, (c, c), 1)
    tril_incl = row >= col
    tril_strict = row > col
    eye = (row == col).astype(F32)
    row_id = lax.broadcasted_iota(jnp.int32, (c, LANES), 0)

    def intra(ci, carry):
        r0 = pl.multiple_of(ci * c, c)
        gam_col, gam_row, beta_all = [], [], []
        for bb in range(nb):
            sm = sm_ref[bb, pl.ds(r0, c), :]
            g_all = -jnp.exp(alog_ref[...]) * jax.nn.softplus(sm + dtb_ref[...])
            beta_all.append(jax.nn.sigmoid(sm))
            gam = g_all
            step = 1
            while step < c:
                gam = gam + jnp.where(row_id >= step, pltpu.roll(gam, step, axis=0), 0.0)
                step *= 2
            gam_col.append(gam)
            gam_row.append(gam.T)
        gc = [gam_col[bb][:, SM_A + h:SM_A + h + 1] for bb, h in probs]
        bc = [beta_all[bb][:, SM_B + h:SM_B + h + 1] for bb, h in probs]
        g_last = [g[c - 1:c, :] for g in gc]
        decay = [jnp.exp(jnp.minimum(g - gam_row[bb][SM_A + h:SM_A + h + 1, :], 0.0))
                 for g, (bb, h) in zip(gc, probs)]
        qh = [q_ref[bb, pl.ds(r0, c), h * d:(h + 1) * d] for bb, h in probs]
        kh = [k_ref[bb, pl.ds(r0, c), h * d:(h + 1) * d] for bb, h in probs]
        vh = [v_ref[bb, pl.ds(r0, c), h * d:(h + 1) * d] for bb, h in probs]
        kq = [_dot_nt(jnp.concatenate([k, q], axis=0), k) for k, q in zip(kh, qh)]
        qk = [jnp.where(tril_incl, x[c:] * dc, 0.0) for x, dc in zip(kq, decay)]
        n_pow = [-jnp.where(tril_strict, b_ * x[:c] * dc, 0.0) for x, b_, dc in zip(kq, bc, decay)]
        t_inv = [eye + n for n in n_pow]
        n_pow = [_dot(n.astype(BF16), n.astype(BF16)) for n in n_pow]
        for _ in range(int(math.log2(c)) - 2):
            prod = [_dot(jnp.concatenate([n, t], axis=0).astype(BF16), n.astype(BF16))
                    for n, t in zip(n_pow, t_inv)]
            n_pow = [p[:c] for p in prod]
            t_inv = [t + p[c:] for t, p in zip(t_inv, prod)]
        t_inv = [t + _dot(t.astype(BF16), n.astype(BF16)) for n, t in zip(n_pow, t_inv)]
        e_gc = [jnp.exp(g) for g in gc]
        kf = [k.astype(F32) for k in kh]
        rhs = [jnp.concatenate([v.astype(F32) * b_, k * (b_ * e)], axis=1)
               for v, k, b_, e in zip(vh, kf, bc, e_gc)]
        uw = [_dot(t.astype(BF16), r.astype(BF16)) for t, r in zip(t_inv, rhs)]
        for i, (bb, h) in enumerate(probs):
            k_dec = kf[i] * jnp.exp(g_last[i] - gc[i])
            u_ref[bb, ci, h] = uw[i][:, :d]
            wq_ref[bb, ci, h] = jnp.concatenate([uw[i][:, d:], qh[i].astype(F32) * e_gc[i]], axis=0).astype(BF16)
            qkk_ref[bb, ci, h] = jnp.concatenate([qk[i], k_dec.T], axis=0).astype(BF16)
            gl_ref[bb, ci, h] = jnp.broadcast_to(jnp.exp(g_last[i]), (1, d))
        return carry

    lax.fori_loop(0, nch, intra, 0)

    def scan(ci, carry):
        r0 = pl.multiple_of(ci * c, c)
        s_old = [s_ref[bb, h] for bb, h in probs]
        m1 = [_dot(wq_ref[bb, ci, h], s.astype(BF16)) for s, (bb, h) in zip(s_old, probs)]
        v_new = [u_ref[bb, ci, h] - m[:c] for m, (bb, h) in zip(m1, probs)]
        m2 = [_dot(qkk_ref[bb, ci, h], v.astype(BF16)) for v, (bb, h) in zip(v_new, probs)]
        for i, (bb, h) in enumerate(probs):
            s_ref[bb, h] = s_old[i] * gl_ref[bb, ci, h] + m2[i][c:]
            o = m1[i][c:] + m2[i][:c]
            o = o * lax.rsqrt(jnp.mean(o * o, axis=-1, keepdims=True) + NORM_EPS) * gn_ref[...]
            z = z_ref[bb, pl.ds(r0, c), h * d:(h + 1) * d].astype(F32)
            o_ref[bb, pl.ds(r0, c), h * d:(h + 1) * d] = (o * (z * jax.nn.sigmoid(z))).astype(o_ref.dtype)
        return carry

    lax.fori_loop(0, nch, scan, 0)


def _gdn_chunk(q, k, v, proj, smalls, a_log, dt_bias, gdn_norm, *, tl):
    b, l, _ = q.shape
    c, d, nh = GDN_CHUNK, GDN_HEAD_DIM, GDN_HEADS
    nch = tl // c
    nb = 2 if b % 2 == 0 else 1
    pad = lambda t, at: jnp.zeros((1, LANES), F32).at[0, at:at + t.shape[0]].set(t.astype(F32))
    qspec = pl.BlockSpec((nb, tl, GDN_WIDTH), lambda bi, i: (bi, i, 0))
    vec = pl.BlockSpec((1, LANES), lambda bi, i: (0, 0))
    return pl.pallas_call(
        functools.partial(_gdn_chunk_body, nch=nch, nb=nb),
        grid=(b // nb, l // tl),
        in_specs=[qspec, qspec, qspec,
                  pl.BlockSpec((nb, tl, GDN_WIDTH), lambda bi, i: (bi, i, COL_Z_A // GDN_WIDTH)),
                  pl.BlockSpec((nb, tl, LANES), lambda bi, i: (bi, i, 0)),
                  vec, vec, vec],
        out_specs=qspec,
        out_shape=jax.ShapeDtypeStruct((b, l, GDN_WIDTH), BF16),
        scratch_shapes=[pltpu.VMEM((nb, nh, d, d), F32),
                        pltpu.VMEM((nb, nch, nh, 2 * c, d), BF16),
                        pltpu.VMEM((nb, nch, nh, c + d, c), BF16),
                        pltpu.VMEM((nb, nch, nh, c, d), F32),
                        pltpu.VMEM((nb, nch, nh, 1, d), F32)],
        compiler_params=_cparams("parallel", "arbitrary"),
        name="gdn_chunk",
    )(q, k, v, proj, smalls, pad(a_log, SM_A), pad(dt_bias, SM_A), gdn_norm.reshape(1, LANES).astype(F32))


def _swap_halves(x, half):
    if 2 * half == LANES:
        return pltpu.roll(x, half, axis=1)
    lane = lax.broadcasted_iota(jnp.int32, x.shape, 1)
    first = (lane % (2 * half)) < half
    return jnp.where(first, pltpu.roll(x, LANES - half, axis=1), pltpu.roll(x, half, axis=1))


def _dsa_prep_body(qk_ref, qi_ref, sm_ref, ca_ref, sa_ref, ci_ref, si_ref,
                   qb_ref, kb_ref, qio_ref, kio_ref):
    ca, sa = ca_ref[...], sa_ref[...]
    for h in range(2 * DSA_HEADS):
        lo = h * DSA_HEAD_DIM
        x = qk_ref[0, :, lo:lo + DSA_HEAD_DIM].astype(F32)
        y = x * ca + _swap_halves(x, DSA_HEAD_DIM // 2) * sa
        if h < DSA_HEADS:
            qb_ref[0, :, lo:lo + DSA_HEAD_DIM] = (y * Q_SCALE).astype(BF16)
        else:
            lo -= DSA_WIDTH
            kb_ref[0, :, lo:lo + DSA_HEAD_DIM] = y.astype(BF16)
    ci, si = ci_ref[...], si_ref[...]
    for p in range(IDX_HEADS * IDX_HEAD_DIM // LANES):
        lo = p * LANES
        x = qi_ref[0, :, lo:lo + LANES].astype(F32)
        y = x * ci + _swap_halves(x, IDX_HEAD_DIM // 2) * si
        qio_ref[0, :, lo:lo + LANES] = y.astype(BF16)
    x = sm_ref[0]
    y = x * ci + _swap_halves(x, IDX_HEAD_DIM // 2) * si
    kio_ref[0] = (y[:, SM_K:] * (IDX_HEAD_DIM ** -0.5)).astype(BF16)


def _dsa_prep(proj, smalls, tabs, *, tl):
    b, l, _ = proj.shape
    w2 = 2 * DSA_WIDTH
    wi = IDX_HEADS * IDX_HEAD_DIM
    tab = pl.BlockSpec((tl, LANES), lambda bi, i: (i, 0))
    big = lambda n: pl.BlockSpec((1, tl, n), lambda bi, i: (bi, i, 0))
    return pl.pallas_call(
        _dsa_prep_body,
        grid=(b, l // tl),
        in_specs=[pl.BlockSpec((1, tl, w2), lambda bi, i: (bi, i, COL_QKV_B // w2)),
                  pl.BlockSpec((1, tl, wi), lambda bi, i: (bi, i, COL_Q_I // wi)),
                  big(LANES), tab, tab, tab, tab],
        out_specs=[big(DSA_WIDTH), big(DSA_WIDTH), big(wi), big(IDX_HEAD_DIM)],
        out_shape=[jax.ShapeDtypeStruct((b, l, DSA_WIDTH), BF16),
                   jax.ShapeDtypeStruct((b, l, DSA_WIDTH), BF16),
                   jax.ShapeDtypeStruct((b, l, wi), BF16),
                   jax.ShapeDtypeStruct((b, l, IDX_HEAD_DIM), BF16)],
        compiler_params=_cparams("parallel", "parallel"),
        name="dsa_prep",
    )(proj, proj, smalls, *tabs)


def _dsa_select_body(qi_ref, sm_ref, ki_ref, bias_in_ref, o_ref, s_ref, *, tq, tc, topk, n_c, tile0):
    del bias_in_ref
    nl = tc // LANES
    half = tq // 2
    q0 = (tile0 + pl.program_id(1)) * tq
    kf = float(topk)
    w_all = sm_ref[0] * (IDX_HEADS ** -0.5)
    rb = min(SELECT_ROW_BLOCK, half)
    rep = lambda col: jnp.broadcast_to(col, (col.shape[0], LANES))
    lane_tiles = lambda x: [x[:, j * LANES:(j + 1) * LANES] for j in range(nl)]

    def score_chunk(k0, carry, last):
        mn, mx, n_pos, n_nonneg = carry
        kc = ki_ref[0, pl.ds(k0, tc), :]
        s = jnp.zeros((tq, tc), F32)
        for h in range(IDX_HEADS):
            qh = qi_ref[0, :, h * IDX_HEAD_DIM:(h + 1) * IDX_HEAD_DIM]
            s = s + w_all[:, SM_W + h:SM_W + h + 1] * jnp.maximum(_dot_nt(qh, kc), 0.0)
        if last:
            qpos = q0 + lax.broadcasted_iota(jnp.int32, (tq, tc), 0)
            causal = k0 + lax.broadcasted_iota(jnp.int32, (tq, tc), 1) <= qpos
            s_lo, s_hi = jnp.where(causal, s, -jnp.inf), jnp.where(causal, s, jnp.inf)
        else:
            s_lo = s_hi = s
        s_ref[:, pl.ds(k0, tc)] = s_lo
        for lo_t, hi_t in zip(lane_tiles(s_lo), lane_tiles(s_hi)):
            mx = jnp.maximum(mx, lo_t)
            mn = jnp.minimum(mn, hi_t)
            n_pos = n_pos + jnp.where(lo_t > 0.0, 1.0, 0.0)
            n_nonneg = n_nonneg + jnp.where(lo_t >= 0.0, 1.0, 0.0)
        return mn, mx, n_pos, n_nonneg

    zeros = jnp.zeros((tq, LANES), F32)
    stats = lax.fori_loop(0, n_c - 1, lambda ci, c: score_chunk(pl.multiple_of(ci * tc, tc), c, False),
                          (zeros + jnp.inf, zeros - jnp.inf, zeros, zeros), unroll=2)
    mn, mx, n_pos, n_nonneg = score_chunk((n_c - 1) * tc, stats, True)
    row_min = rep(jnp.min(mn, axis=1, keepdims=True))
    row_max = rep(jnp.max(mx, axis=1, keepdims=True))
    n_pos = rep(jnp.sum(n_pos, axis=1, keepdims=True))
    n_nonneg = rep(jnp.sum(n_nonneg, axis=1, keepdims=True))

    def reduce_keys(fn, init, combine, r_lo, r_hi, unrolled):
        accs = []
        for r0 in range(r_lo, r_hi, rb):
            def chunk(k0, acc, r0=r0):
                for j in range(nl):
                    acc = combine(acc, fn(s_ref[r0:r0 + rb, pl.ds(k0 + j * LANES, LANES)], r0 - r_lo))
                return acc
            acc = jnp.full((rb, LANES), init, F32)
            if unrolled:
                for ci in range(n_c):
                    acc = chunk(ci * tc, acc)
            else:
                acc = lax.fori_loop(0, n_c, lambda ci, a: chunk(pl.multiple_of(ci * tc, tc), a), acc)
            accs.append(acc)
        return jnp.concatenate(accs, axis=0)

    def count_acc(op, th, r_lo, r_hi, unrolled=True):
        return reduce_keys(lambda t, r: jnp.where(op(t, th[r:r + rb]), 1.0, 0.0), 0.0, jnp.add, r_lo, r_hi, unrolled)

    lane_max = lambda x: rep(jnp.max(x, axis=1, keepdims=True))
    lane_sum = lambda x: rep(jnp.sum(x, axis=1, keepdims=True))
    ge = lambda a, b: a >= b
    count = lambda op, th: lane_sum(count_acc(op, th, 0, tq, unrolled=False))
    is_open = lambda lo, hi: jnp.max(hi - lo) > 0.0
    middle = lambda lo, hi: lo + (hi - lo) * 0.5

    def narrow(lo, hi, mid, cnt):
        hit = cnt == kf
        above = cnt > kf
        return jnp.where(hit | above, mid, lo), jnp.where(hit | jnp.logical_not(above), mid, hi)

    n_causal = (q0 + 1 + lax.broadcasted_iota(jnp.int32, (tq, LANES), 0)).astype(F32)
    few = n_causal <= kf
    zero_tied = (n_pos < kf) & (n_nonneg >= kf)
    starts_closed = few | zero_tied
    closed_at = jnp.where(few, -jnp.finfo(F32).max, 0.0)
    positive = n_pos >= kf
    lo = jnp.where(starts_closed, closed_at, jnp.where(positive, jnp.maximum(row_min, 0.0), row_min))
    hi = jnp.where(starts_closed, closed_at, jnp.where(positive, row_max, jnp.minimum(row_max, 0.0)))
    surplus = jnp.where((zero_tied & (n_nonneg > kf)) | (jnp.logical_not(starts_closed) & (lo >= hi)), 1.0, 0.0)

    def search_cond(c):
        return (c[0] < BISECT_STEPS) & (c[7] > 0)

    def search_body(c):
        it, lo_a, hi_a, lo_b, hi_b, acc_b, mid_b, _ = c
        go_on = (is_open(lo_a, hi_a) | is_open(lo_b, hi_b)).astype(jnp.int32)
        mid_a = middle(lo_a, hi_a)
        acc_a = count_acc(ge, mid_a, 0, half)
        lo_b, hi_b = narrow(lo_b, hi_b, mid_b, lane_sum(acc_b))
        mid_b = middle(lo_b, hi_b)
        acc_b = count_acc(ge, mid_b, half, tq)
        lo_a, hi_a = narrow(lo_a, hi_a, mid_a, lane_sum(acc_a))
        return it + 1, lo_a, hi_a, lo_b, hi_b, acc_b, mid_b, go_on

    lo_a, hi_a, lo_b, hi_b = lo[:half], hi[:half], lo[half:], hi[half:]
    mid_b = middle(lo_b, hi_b)
    _, lo_a, hi_a, lo_b, hi_b, acc_b, mid_b, _ = lax.while_loop(
        search_cond, search_body,
        (jnp.int32(0), lo_a, hi_a, lo_b, hi_b, count_acc(ge, mid_b, half, tq), mid_b, jnp.int32(1)))
    lo_b, hi_b = narrow(lo_b, hi_b, mid_b, lane_sum(acc_b))
    lo = jnp.concatenate([lo_a, lo_b], axis=0)
    hi = jnp.concatenate([hi_a, hi_b], axis=0)

    def peel_cond(c):
        return (c[0] < topk + 2) & (c[4] > 0)

    def peel_body(c):
        it, lo, hi, surplus, _ = c
        still = hi > lo
        widen = jnp.where(it == 0, jnp.finfo(F32).max, 0.0)
        below = jnp.where(hi == row_max, hi + widen, hi)
        top = lane_max(reduce_keys(
            lambda t, r: jnp.where((t >= lo[r:r + rb]) & (t < below[r:r + rb]), t, -jnp.inf),
            -jnp.inf, jnp.maximum, 0, tq, False))
        n_top = count(ge, top)
        found = still & (n_top >= kf)
        lo = jnp.where(found, top, lo)
        hi = jnp.where(still, top, hi)
        surplus = jnp.where(found & (n_top > kf), 1.0, surplus)
        return it + 1, lo, hi, surplus, is_open(lo, hi).astype(jnp.int32)

    _, thr, _, surplus, _ = lax.while_loop(
        peel_cond, peel_body, (jnp.int32(0), lo, hi, surplus, is_open(lo, hi).astype(jnp.int32)))

    for r0 in range(0, tq, rb):
        rows = slice(r0, r0 + rb)
        thr_r = thr[rows]
        has_excess = jnp.max(surplus[rows]) > 0.0

        def write_plain(ci, carry, rows=rows, thr_r=thr_r):
            k0 = pl.multiple_of(ci * tc, tc)
            for j in range(nl):
                cols = pl.ds(k0 + j * LANES, LANES)
                o_ref[0, rows, cols] = jnp.where(s_ref[rows, cols] >= thr_r, 0.0, MASK_BIAS).astype(o_ref.dtype)
            return carry

        @pl.when(jnp.logical_not(has_excess))
        def _(write_plain=write_plain):
            lax.fori_loop(0, n_c, write_plain, 0)

        @pl.when(has_excess)
        def _(r0=r0, rows=rows, thr_r=thr_r):
            n_gt = lane_sum(count_acc(lambda a, b: a > b, thr_r, r0, r0 + rb, unrolled=False))
            need_c = (kf - n_gt)[:, 0:1]
            thr_c = thr_r[:, 0:1]
            r = lax.broadcasted_iota(jnp.int32, (tc, tc), 0)
            cidx = lax.broadcasted_iota(jnp.int32, (tc, tc), 1)
            upper = jnp.where(r < cidx, 1.0, 0.0).astype(BF16)

            def write_ties(ci, seen):
                k0 = pl.multiple_of(ci * tc, tc)
                sc = s_ref[rows, pl.ds(k0, tc)]
                eq = sc == thr_c
                before = seen + _dot(jnp.where(eq, 1.0, 0.0).astype(BF16), upper)
                keep = (sc > thr_c) | (eq & (before < need_c))
                o_ref[0, rows, pl.ds(k0, tc)] = jnp.where(keep, 0.0, MASK_BIAS).astype(o_ref.dtype)
                return seen + jnp.sum(jnp.where(eq, 1.0, 0.0), axis=1, keepdims=True)

            lax.fori_loop(0, n_c, write_ties, jnp.zeros((rb, 1), F32))


def _dsa_select(qi, smalls, ki, bias, *, tq, tc, topk):
    b, l, wi = qi.shape
    per_call = tc // tq
    for n_c in range(1, l // tc + 1):
        tile0 = (n_c - 1) * per_call
        bias = pl.pallas_call(
            functools.partial(_dsa_select_body, tq=tq, tc=tc, topk=topk, n_c=n_c, tile0=tile0),
            grid=(b, per_call),
            in_specs=[pl.BlockSpec((1, tq, wi), lambda bi, t, tile0=tile0: (bi, tile0 + t, 0)),
                      pl.BlockSpec((1, tq, LANES), lambda bi, t, tile0=tile0: (bi, tile0 + t, 0)),
                      pl.BlockSpec((1, n_c * tc, IDX_HEAD_DIM), lambda bi, t: (bi, 0, 0)),
                      pl.BlockSpec(memory_space=pl.ANY)],
            out_specs=pl.BlockSpec((1, tq, n_c * tc), lambda bi, t, tile0=tile0: (bi, tile0 + t, 0)),
            out_shape=jax.ShapeDtypeStruct((b, l, l), BF16),
            scratch_shapes=[pltpu.VMEM((tq, n_c * tc), F32)],
            input_output_aliases={3: 0},
            compiler_params=_cparams("parallel", "parallel"),
            name=f"dsa_select_{n_c}",
        )(qi, smalls, ki, bias)
    return bias


def _dsa_attn_body(qt_ref, kt_ref, q_ref, k_ref, v_ref, b_ref, o_ref, m_ref, acc_ref):
    step = pl.program_id(1)
    qi, ki = qt_ref[step], kt_ref[step]
    d = DSA_HEAD_DIM

    @pl.when(ki == 0)
    def _():
        m_ref[...] = jnp.full_like(m_ref, MASK_BIAS)
        acc_ref[...] = jnp.zeros_like(acc_ref)

    bias = b_ref[0]
    ones = jnp.ones((v_ref.shape[1], d), BF16)
    for h0 in range(0, DSA_HEADS, ATTN_HEAD_GROUP):
        heads = range(h0, h0 + ATTN_HEAD_GROUP)
        s = [_dot_nt(q_ref[0, :, h * d:(h + 1) * d], k_ref[0, :, h * d:(h + 1) * d]).astype(BF16) + bias
             for h in heads]
        m_old = [m_ref[h] for h in heads]
        m_new = [jnp.maximum(mo, jnp.max(x, axis=-1, keepdims=True).astype(F32)) for mo, x in zip(m_old, s)]
        alpha = [jnp.exp2(mo - mn) for mo, mn in zip(m_old, m_new)]
        p = [jnp.exp2(x - mn[:, 0:1].astype(BF16)) for x, mn in zip(s, m_new)]
        upd = [_dot(x, jnp.concatenate([v_ref[0, :, h * d:(h + 1) * d], ones], axis=1)) for x, h in zip(p, heads)]
        for i, h in enumerate(heads):
            acc_ref[h, :, :d] = alpha[i] * acc_ref[h, :, :d] + upd[i][:, :d]
            acc_ref[h, :, d:] = alpha[i] * acc_ref[h, :, d:] + upd[i][:, d:]
            m_ref[h] = m_new[i]

    @pl.when(ki == qi)
    def _():
        for h in range(DSA_HEADS):
            lo = h * d
            acc = acc_ref[h]
            o_ref[0, :, lo:lo + d] = (acc[:, :d] / acc[:, d:]).astype(o_ref.dtype)


def _dsa_attention(qb, kb, proj, bias, *, t):
    b, l, w = qb.shape
    n = l // t
    pairs = [(i, j) for i in range(n) for j in range(i + 1)]
    q_tile = jnp.asarray([p[0] for p in pairs], jnp.int32)
    k_tile = jnp.asarray([p[1] for p in pairs], jnp.int32)
    v_col = (COL_QKV_B + 2 * DSA_WIDTH) // DSA_WIDTH
    q_idx = lambda bi, s, qt, kt: (bi, qt[s], 0)
    kv_idx = lambda bi, s, qt, kt: (bi, kt[s], 0)
    return pl.pallas_call(
        _dsa_attn_body,
        grid_spec=pltpu.PrefetchScalarGridSpec(
            num_scalar_prefetch=2,
            grid=(b, len(pairs)),
            in_specs=[pl.BlockSpec((1, t, w), q_idx),
                      pl.BlockSpec((1, t, w), kv_idx),
                      pl.BlockSpec((1, t, w), lambda bi, s, qt, kt: (bi, kt[s], v_col)),
                      pl.BlockSpec((1, t, t), lambda bi, s, qt, kt: (bi, qt[s], kt[s]))],
            out_specs=pl.BlockSpec((1, t, w), q_idx),
            scratch_shapes=[pltpu.VMEM((DSA_HEADS, t, LANES), F32),
                            pltpu.VMEM((DSA_HEADS, t, 2 * DSA_HEAD_DIM), F32)]),
        out_shape=jax.ShapeDtypeStruct((b, l, w), BF16),
        compiler_params=_cparams("parallel", "arbitrary"),
        name="dsa_attention",
    )(q_tile, k_tile, qb, kb, proj, bias)


def _rope_tables(seq_len):
    def tab(dim):
        inv_freq = 1.0 / (ROPE_THETA ** (jnp.arange(0, dim, 2, dtype=F32) / dim))
        split = math.gcd(seq_len, ROPE_SPLIT)
        ang_lo = jnp.arange(split, dtype=F32)[None, :, None] * inv_freq
        ang_hi = (jnp.arange(seq_len // split, dtype=F32) * split)[:, None, None] * inv_freq
        c = (jnp.cos(ang_hi) * jnp.cos(ang_lo) - jnp.sin(ang_hi) * jnp.sin(ang_lo)).reshape(seq_len, dim // 2)
        s = (jnp.sin(ang_hi) * jnp.cos(ang_lo) + jnp.cos(ang_hi) * jnp.sin(ang_lo)).reshape(seq_len, dim // 2)
        return jnp.concatenate([c, c], axis=1), jnp.concatenate([-s, s], axis=1)
    ca, sa = tab(DSA_HEAD_DIM)
    c64, s64 = tab(IDX_HEAD_DIM)
    ci = jnp.concatenate([c64, c64], axis=1)
    si = jnp.concatenate([s64, s64], axis=1)
    return ca, sa, ci, si


def _regroup_w_in(w):
    sizes = (3 * GDN_WIDTH, GDN_WIDTH, GDN_HEADS, GDN_HEADS, 3 * DSA_WIDTH,
             IDX_HEADS * IDX_HEAD_DIM, IDX_HEAD_DIM, IDX_HEADS, 1024, 1024)
    parts, start = [], 0
    for s in sizes:
        parts.append(w[..., start:start + s])
        start += s
    qkv_a, z_a, a_a, b_a, qkv_b, q_i, k_i, w_i, g_a, g_b = parts
    main = jnp.concatenate([qkv_a, z_a, qkv_b, g_a, g_b, q_i], axis=-1).astype(BF16)
    pad = jnp.zeros(w.shape[:-1] + (SM_K - SM_W - IDX_HEADS,), w.dtype)
    smalls = jnp.concatenate([a_a, b_a, w_i, pad, k_i], axis=-1)
    return main, smalls


def kernel(x, norm_mix, w_in, conv_w, a_log, dt_bias, gdn_norm, w_out_gdn, w_out_dsa, w_o,
           norm_ffn, w_gate_up, w_down, norm_final):
    b, l, dm = x.shape
    depth = w_in.shape[0]
    m = b * l
    topk = min(TOPK_MAX, l // 4)
    tabs = _rope_tables(l)
    tm = min(1024, m)
    t_attn = min(512, l)
    tq_sel = min(256, l)
    h = x.reshape(m, dm)
    bias = jnp.full((b, l, l), MASK_BIAS, BF16)
    w_main, w_smalls = _regroup_w_in(w_in)
    w_out_gdn, w_out_dsa, w_o, w_gate_up, w_down = (
        w.astype(BF16) for w in (w_out_gdn, w_out_dsa, w_o, w_gate_up, w_down))
    for layer in range(depth):
        proj = _norm_matmul(h, norm_mix[layer], w_main, layer, tm=min(512, m), tn=MAIN_COLS // 4, out_dtype=BF16)
        smalls = _norm_matmul(h, norm_mix[layer], w_smalls, layer, tm=tm, tn=LANES, out_dtype=F32, exact=True)
        proj = proj.reshape(b, l, MAIN_COLS)
        smalls = smalls.reshape(b, l, LANES)

        q_a, k_a, v_a = _gdn_prep(proj, conv_w[layer], tl=min(256, l))
        o_a = _gdn_chunk(q_a, k_a, v_a, proj, smalls, a_log[layer], dt_bias[layer], gdn_norm[layer],
                         tl=min(512, l))

        q_b, k_b, q_i, k_i = _dsa_prep(proj, smalls, tabs, tl=min(512, l))
        bias = _dsa_select(q_i, smalls, k_i, bias, tq=tq_sel, tc=min(512, l), topk=topk)
        o_b = _dsa_attention(q_b, k_b, proj, bias, t=t_attn)

        h = _mixer_out(o_a.reshape(m, GDN_WIDTH), o_b.reshape(m, DSA_WIDTH), w_out_gdn, w_out_dsa, w_o, layer,
                       proj.reshape(m, MAIN_COLS), h, tm=min(512, m))
        last = layer == depth - 1
        h = _ffn(h, norm_ffn[layer], w_gate_up, w_down, layer,
                 tm=min(512, m), final_gain=norm_final if last else None)
    return h.reshape(b, l, dm)
```

```python
import functools
import math

import jax
import jax.numpy as jnp
from jax import lax
from jax.experimental import pallas as pl
from jax.experimental.pallas import tpu as pltpu

F32 = jnp.float32
BF16 = jnp.bfloat16

D_MODEL = 1024
GDN_HEADS = 8
GDN_HEAD_DIM = 128
GDN_WIDTH = GDN_HEADS * GDN_HEAD_DIM
CONV_K = 4
GDN_CHUNK = 64
DSA_HEADS = 8
DSA_HEAD_DIM = 128
DSA_WIDTH = DSA_HEADS * DSA_HEAD_DIM
IDX_HEADS = 8
IDX_HEAD_DIM = 64
TOPK_MAX = 256
ROPE_THETA = 10000.0
ROPE_SPLIT = 64
NORM_EPS = 1e-6

LANES = 128
VMEM_LIMIT = 48 * 1024 * 1024
ROW_TILE = 512
SMALLS_ROW_TILE = 1024
CONV_TIME_BLOCK = 256
GDN_TIME_BLOCK = 512
ROPE_TIME_BLOCK = 512
SELECT_Q_TILE = 256
SELECT_KEY_CHUNK = 512
ATTN_TILE = 512
PROJ_COL_TILES = 4
BISECT_STEPS = 32
SELECT_ROW_BLOCK = 64
ATTN_HEAD_GROUP = 4
MASK_BIAS = -(2.0 ** 100)
Q_SCALE = DSA_HEAD_DIM ** -0.5 * math.log2(math.e)

COL_QKV_A = 0
COL_Z_A = 3 * GDN_WIDTH
COL_QKV_B = COL_Z_A + GDN_WIDTH
COL_G_A = COL_QKV_B + 3 * DSA_WIDTH
COL_Q_I = COL_G_A + 2 * D_MODEL
MAIN_COLS = COL_Q_I + IDX_HEADS * IDX_HEAD_DIM
SM_A, SM_B, SM_W, SM_K = 0, 8, 16, 64


def _cparams(*sem):
    return pltpu.CompilerParams(dimension_semantics=sem, vmem_limit_bytes=VMEM_LIMIT)


def _rmsnorm_rows(x, gain):
    ms = jnp.mean(x * x, axis=-1, keepdims=True)
    return x * lax.rsqrt(ms + NORM_EPS) * gain


def _dot(a, b):
    return jnp.dot(a, b, preferred_element_type=F32)


def _dot_nt(a, b):
    return lax.dot_general(a, b, (((1,), (1,)), ((), ())), preferred_element_type=F32)


def _resident(shape, layer=None):
    if layer is None:
        return pl.BlockSpec(shape, lambda i, j: (0,) * len(shape), pipeline_mode=pl.Buffered(1))
    return pl.BlockSpec((None,) + shape, lambda i, j: (layer,) + (0,) * len(shape), pipeline_mode=pl.Buffered(1))


def _norm_mm_body(x_ref, g_ref, w_ref, o_ref, xn_ref, *, tn, exact):
    @pl.when(pl.program_id(1) == 0)
    def _():
        xn_ref[...] = _rmsnorm_rows(x_ref[...], g_ref[...]).astype(xn_ref.dtype)

    w = w_ref[:, pl.ds(pl.multiple_of(pl.program_id(1) * tn, tn), tn)]
    if exact:
        x = xn_ref[...]
        x_hi, w_hi = x.astype(BF16), w.astype(BF16)
        x_lo = (x - x_hi.astype(F32)).astype(BF16)
        w_lo = (w - w_hi.astype(F32)).astype(BF16)
        o = _dot(x_hi, w_hi) + (_dot(x_hi, w_lo) + _dot(x_lo, w_hi))
    else:
        o = _dot(xn_ref[...], w)
    o_ref[...] = o.astype(o_ref.dtype)


def _norm_matmul(x, gain, w, layer, *, tm, tn, out_dtype, exact=False):
    m, k = x.shape
    n = w.shape[2]
    return pl.pallas_call(
        functools.partial(_norm_mm_body, tn=tn, exact=exact),
        grid=(m // tm, n // tn),
        in_specs=[pl.BlockSpec((tm, k), lambda i, j: (i, 0)),
                  _resident((1, k)),
                  _resident((k, n), layer)],
        out_specs=pl.BlockSpec((tm, tn), lambda i, j: (i, j)),
        out_shape=jax.ShapeDtypeStruct((m, n), out_dtype),
        scratch_shapes=[pltpu.VMEM((tm, k), w.dtype)],
        compiler_params=_cparams("parallel", "arbitrary"),
        name="norm_matmul_exact" if exact else "norm_matmul",
    )(x, gain.reshape(1, k), w)


def _resident1(shape, layer=None):
    if layer is None:
        return pl.BlockSpec(shape, lambda i: (0,) * len(shape), pipeline_mode=pl.Buffered(1))
    return pl.BlockSpec((None,) + shape, lambda i: (layer,) + (0,) * len(shape), pipeline_mode=pl.Buffered(1))


def _ffn_body(x_ref, g_ref, wgu_ref, wd_ref, *rest, f, final_norm):
    if final_norm:
        gf_ref, o_ref = rest
    else:
        (o_ref,) = rest
    x = x_ref[...]
    xn = _rmsnorm_rows(x, g_ref[...]).astype(BF16)
    g = _dot(xn, wgu_ref[:, :f])
    u = _dot(xn, wgu_ref[:, f:])
    act = (g * jax.nn.sigmoid(g) * u).astype(BF16)
    h = x + _dot(act, wd_ref[...])
    if final_norm:
        h = _rmsnorm_rows(h, gf_ref[...])
    o_ref[...] = h


def _ffn(x, gain, w_gate_up, w_down, layer, *, tm, final_gain=None):
    m, k = x.shape
    f = w_down.shape[1]
    final_norm = final_gain is not None
    in_specs = [pl.BlockSpec((tm, k), lambda i: (i, 0)),
                _resident1((1, k)), _resident1((k, 2 * f), layer), _resident1((f, k), layer)]
    args = [x, gain.reshape(1, k), w_gate_up, w_down]
    if final_norm:
        in_specs.append(_resident1((1, k)))
        args.append(final_gain.reshape(1, k))
    return pl.pallas_call(
        functools.partial(_ffn_body, f=f, final_norm=final_norm),
        grid=(m // tm,),
        in_specs=in_specs,
        out_specs=pl.BlockSpec((tm, k), lambda i: (i, 0)),
        out_shape=jax.ShapeDtypeStruct((m, k), F32),
        compiler_params=_cparams("parallel"),
        name="ffn_norm" if final_norm else "ffn",
    )(*args)


def _mixer_out_body(oa_ref, ob_ref, wa_ref, wb_ref, wo_ref, ga_ref, gb_ref, h_ref, o_ref):
    ya = _dot(oa_ref[...], wa_ref[...])
    yb = _dot(ob_ref[...], wb_ref[...])
    ga = jax.nn.sigmoid(ga_ref[...].astype(F32))
    gb = jax.nn.sigmoid(gb_ref[...].astype(F32))
    merged = (ga * ya + gb * yb).astype(BF16)
    o_ref[...] = h_ref[...] + _dot(merged, wo_ref[...])


def _mixer_out(oa, ob, wa, wb, wo, layer, proj, h, *, tm):
    m, k = oa.shape
    n = wa.shape[2]
    row = lambda w: pl.BlockSpec((tm, w), lambda i: (i, 0))
    return pl.pallas_call(
        _mixer_out_body,
        grid=(m // tm,),
        in_specs=[row(k), row(k), _resident1((k, n), layer), _resident1((k, n), layer), _resident1((n, n), layer),
                  pl.BlockSpec((tm, n), lambda i: (i, COL_G_A // n)),
                  pl.BlockSpec((tm, n), lambda i: (i, COL_G_A // n + 1)),
                  row(n)],
        out_specs=row(n),
        out_shape=jax.ShapeDtypeStruct((m, n), F32),
        compiler_params=_cparams("parallel"),
        name="mixer_out",
    )(oa, ob, wa, wb, wo, proj, proj, h)


HALO = 16
CONV_ROWS = 128


def _gdn_prep_body(x_ref, halo_ref, cw_ref, q_ref, k_ref, v_ref, xs_ref, *, tl):
    i = pl.program_id(1)
    sub = CONV_ROWS
    xs_ref[0:HALO, :] = jnp.where(i == 0, jnp.zeros_like(halo_ref[0]), halo_ref[0])
    xs_ref[HALO:HALO + tl, :] = x_ref[0]
    t = lax.broadcasted_iota(jnp.int32, ((CONV_K - 1) * sub, HALO + sub), 0)
    c = lax.broadcasted_iota(jnp.int32, ((CONV_K - 1) * sub, HALO + sub), 1)
    shift = jnp.where(c == t % sub + HALO - (CONV_K - 1) + t // sub, 1.0, 0.0).astype(BF16)
    for r0 in range(0, tl, sub):
        win = xs_ref[r0:r0 + HALO + sub, :]
        z = _dot(shift, win)
        y = cw_ref[CONV_K - 1:CONV_K, :] * win[HALO:, :].astype(F32)
        for j in range(CONV_K - 1):
            y = y + cw_ref[j:j + 1, :] * z[j * sub:(j + 1) * sub]
        y = y * jax.nn.sigmoid(y)
        rows = slice(r0, r0 + sub)
        for h in range(GDN_HEADS):
            lo = h * GDN_HEAD_DIM
            qh = y[:, lo:lo + GDN_HEAD_DIM]
            kh = y[:, GDN_WIDTH + lo:GDN_WIDTH + lo + GDN_HEAD_DIM]
            qn = qh * lax.rsqrt(jnp.sum(qh * qh, axis=-1, keepdims=True) + NORM_EPS)
            kn = kh * lax.rsqrt(jnp.sum(kh * kh, axis=-1, keepdims=True) + NORM_EPS)
            q_ref[0, rows, lo:lo + GDN_HEAD_DIM] = (qn * (GDN_HEAD_DIM ** -0.5)).astype(q_ref.dtype)
            k_ref[0, rows, lo:lo + GDN_HEAD_DIM] = kn.astype(k_ref.dtype)
        v_ref[0, rows, :] = y[:, 2 * GDN_WIDTH:].astype(v_ref.dtype)


def _gdn_prep(proj, conv_w, *, tl):
    b, l, _ = proj.shape
    c = 3 * GDN_WIDTH
    r = tl // HALO
    out = jax.ShapeDtypeStruct((b, l, GDN_WIDTH), BF16)
    ospec = pl.BlockSpec((1, tl, GDN_WIDTH), lambda bi, i: (bi, i, 0))
    return pl.pallas_call(
        functools.partial(_gdn_prep_body, tl=tl),
        grid=(b, l // tl),
        in_specs=[pl.BlockSpec((1, tl, c), lambda bi, i: (bi, i, 0)),
                  pl.BlockSpec((1, HALO, c), lambda bi, i: (bi, jnp.maximum(i * r - 1, 0), 0)),
                  pl.BlockSpec((CONV_K, c), lambda bi, i: (0, 0))],
        out_specs=[ospec, ospec, ospec],
        out_shape=[out, out, out],
        scratch_shapes=[pltpu.VMEM((HALO + tl, c), BF16)],
        compiler_params=_cparams("parallel", "parallel"),
        name="gdn_prep",
    )(proj, proj, conv_w)


def _gdn_chunk_body(q_ref, k_ref, v_ref, z_ref, sm_ref, alog_ref, dtb_ref, gn_ref, o_ref,
                    s_ref, wq_ref, qkk_ref, u_ref, gl_ref, *, nch, nb):
    c = GDN_CHUNK
    d = GDN_HEAD_DIM
    probs = [(bb, h) for bb in range(nb) for h in range(GDN_HEADS)]

    @pl.when(pl.program_id(1) == 0)
    def _():
        s_ref[...] = jnp.zeros_like(s_ref)

    row = lax.broadcasted_iota(jnp.int32, (c, c), 0)
    col = lax.broadcasted_iota(jnp.int32, (c, c), 1)
    tril_incl = row >= col
    tril_strict = row > col
    eye = (row == col).astype(F32)
    row_id = lax.broadcasted_iota(jnp.int32, (c, LANES), 0)

    def intra(ci, carry):
        r0 = pl.multiple_of(ci * c, c)
        gam_col, gam_row, beta_all = [], [], []
        for bb in range(nb):
            sm = sm_ref[bb, pl.ds(r0, c), :]
            g_all = -jnp.exp(alog_ref[...]) * jax.nn.softplus(sm + dtb_ref[...])
            beta_all.append(jax.nn.sigmoid(sm))
            gam = g_all
            step = 1
            while step < c:
                gam = gam + jnp.where(row_id >= step, pltpu.roll(gam, step, axis=0), 0.0)
                step *= 2
            gam_col.append(gam)
            gam_row.append(gam.T)
        gc = [gam_col[bb][:, SM_A + h:SM_A + h + 1] for bb, h in probs]
        bc = [beta_all[bb][:, SM_B + h:SM_B + h + 1] for bb, h in probs]
        g_last = [g[c - 1:c, :] for g in gc]
        decay = [jnp.exp(jnp.minimum(g - gam_row[bb][SM_A + h:SM_A + h + 1, :], 0.0))
                 for g, (bb, h) in zip(gc, probs)]
        qh = [q_ref[bb, pl.ds(r0, c), h * d:(h + 1) * d] for bb, h in probs]
        kh = [k_ref[bb, pl.ds(r0, c), h * d:(h + 1) * d] for bb, h in probs]
        vh = [v_ref[bb, pl.ds(r0, c), h * d:(h + 1) * d] for bb, h in probs]
        kq = [_dot_nt(jnp.concatenate([k, q], axis=0), k) for k, q in zip(kh, qh)]
        qk = [jnp.where(tril_incl, x[c:] * dc, 0.0) for x, dc in zip(kq, decay)]
        n_pow = [-jnp.where(tril_strict, b_ * x[:c] * dc, 0.0) for x, b_, dc in zip(kq, bc, decay)]
        t_inv = [eye + n for n in n_pow]
        n_pow = [_dot(n.astype(BF16), n.astype(BF16)) for n in n_pow]
        for _ in range(int(math.log2(c)) - 2):
            prod = [_dot(jnp.concatenate([n, t], axis=0).astype(BF16), n.astype(BF16))
                    for n, t in zip(n_pow, t_inv)]
            n_pow = [p[:c] for p in prod]
            t_inv = [t + p[c:] for t, p in zip(t_inv, prod)]
        t_inv = [t + _dot(t.astype(BF16), n.astype(BF16)) for n, t in zip(n_pow, t_inv)]
        e_gc = [jnp.exp(g) for g in gc]
        kf = [k.astype(F32) for k in kh]
        rhs = [jnp.concatenate([v.astype(F32) * b_, k * (b_ * e)], axis=1)
               for v, k, b_, e in zip(vh, kf, bc, e_gc)]
        uw = [_dot(t.astype(BF16), r.astype(BF16)) for t, r in zip(t_inv, rhs)]
        for i, (bb, h) in enumerate(probs):
            k_dec = kf[i] * jnp.exp(g_last[i] - gc[i])
            u_ref[bb, ci, h] = uw[i][:, :d]
            wq_ref[bb, ci, h] = jnp.concatenate([uw[i][:, d:], qh[i].astype(F32) * e_gc[i]], axis=0).astype(BF16)
            qkk_ref[bb, ci, h] = jnp.concatenate([qk[i], k_dec.T], axis=0).astype(BF16)
            gl_ref[bb, ci, h] = jnp.broadcast_to(jnp.exp(g_last[i]), (1, d))
        return carry

    lax.fori_loop(0, nch, intra, 0)

    def scan(ci, carry):
        r0 = pl.multiple_of(ci * c, c)
        s_old = [s_ref[bb, h] for bb, h in probs]
        m1 = [_dot(wq_ref[bb, ci, h], s.astype(BF16)) for s, (bb, h) in zip(s_old, probs)]
        v_new = [u_ref[bb, ci, h] - m[:c] for m, (bb, h) in zip(m1, probs)]
        m2 = [_dot(qkk_ref[bb, ci, h], v.astype(BF16)) for v, (bb, h) in zip(v_new, probs)]
        for i, (bb, h) in enumerate(probs):
            s_ref[bb, h] = s_old[i] * gl_ref[bb, ci, h] + m2[i][c:]
            o = m1[i][c:] + m2[i][:c]
            o = o * lax.rsqrt(jnp.mean(o * o, axis=-1, keepdims=True) + NORM_EPS) * gn_ref[...]
            z = z_ref[bb, pl.ds(r0, c), h * d:(h + 1) * d].astype(F32)
            o_ref[bb, pl.ds(r0, c), h * d:(h + 1) * d] = (o * (z * jax.nn.sigmoid(z))).astype(o_ref.dtype)
        return carry

    lax.fori_loop(0, nch, scan, 0)


def _gdn_chunk(q, k, v, proj, smalls, a_log, dt_bias, gdn_norm, *, tl):
    b, l, _ = q.shape
    c, d, nh = GDN_CHUNK, GDN_HEAD_DIM, GDN_HEADS
    nch = tl // c
    nb = 2 if b % 2 == 0 else 1
    pad = lambda t, at: jnp.zeros((1, LANES), F32).at[0, at:at + t.shape[0]].set(t.astype(F32))
    qspec = pl.BlockSpec((nb, tl, GDN_WIDTH), lambda bi, i: (bi, i, 0))
    vec = pl.BlockSpec((1, LANES), lambda bi, i: (0, 0))
    return pl.pallas_call(
        functools.partial(_gdn_chunk_body, nch=nch, nb=nb),
        grid=(b // nb, l // tl),
        in_specs=[qspec, qspec, qspec,
                  pl.BlockSpec((nb, tl, GDN_WIDTH), lambda bi, i: (bi, i, COL_Z_A // GDN_WIDTH)),
                  pl.BlockSpec((nb, tl, LANES), lambda bi, i: (bi, i, 0)),
                  vec, vec, vec],
        out_specs=qspec,
        out_shape=jax.ShapeDtypeStruct((b, l, GDN_WIDTH), BF16),
        scratch_shapes=[pltpu.VMEM((nb, nh, d, d), F32),
                        pltpu.VMEM((nb, nch, nh, 2 * c, d), BF16),
                        pltpu.VMEM((nb, nch, nh, c + d, c), BF16),
                        pltpu.VMEM((nb, nch, nh, c, d), F32),
                        pltpu.VMEM((nb, nch, nh, 1, d), F32)],
        compiler_params=_cparams("parallel", "arbitrary"),
        name="gdn_chunk",
    )(q, k, v, proj, smalls, pad(a_log, SM_A), pad(dt_bias, SM_A), gdn_norm.reshape(1, LANES).astype(F32))


def _swap_halves(x, half):
    if 2 * half == LANES:
        return pltpu.roll(x, half, axis=1)
    lane = lax.broadcasted_iota(jnp.int32, x.shape, 1)
    first = (lane % (2 * half)) < half
    return jnp.where(first, pltpu.roll(x, LANES - half, axis=1), pltpu.roll(x, half, axis=1))


def _dsa_prep_body(qk_ref, qi_ref, sm_ref, ca_ref, sa_ref, ci_ref, si_ref,
                   qb_ref, kb_ref, qio_ref, kio_ref):
    ca, sa = ca_ref[...], sa_ref[...]
    for h in range(2 * DSA_HEADS):
        lo = h * DSA_HEAD_DIM
        x = qk_ref[0, :, lo:lo + DSA_HEAD_DIM].astype(F32)
        y = x * ca + _swap_halves(x, DSA_HEAD_DIM // 2) * sa
        if h < DSA_HEADS:
            qb_ref[0, :, lo:lo + DSA_HEAD_DIM] = (y * Q_SCALE).astype(BF16)
        else:
            lo -= DSA_WIDTH
            kb_ref[0, :, lo:lo + DSA_HEAD_DIM] = y.astype(BF16)
    ci, si = ci_ref[...], si_ref[...]
    for p in range(IDX_HEADS * IDX_HEAD_DIM // LANES):
        lo = p * LANES
        x = qi_ref[0, :, lo:lo + LANES].astype(F32)
        y = x * ci + _swap_halves(x, IDX_HEAD_DIM // 2) * si
        qio_ref[0, :, lo:lo + LANES] = y.astype(BF16)
    x = sm_ref[0]
    y = x * ci + _swap_halves(x, IDX_HEAD_DIM // 2) * si
    kio_ref[0] = (y[:, SM_K:] * (IDX_HEAD_DIM ** -0.5)).astype(BF16)


def _dsa_prep(proj, smalls, tabs, *, tl):
    b, l, _ = proj.shape
    w2 = 2 * DSA_WIDTH
    wi = IDX_HEADS * IDX_HEAD_DIM
    tab = pl.BlockSpec((tl, LANES), lambda bi, i: (i, 0))
    big = lambda n: pl.BlockSpec((1, tl, n), lambda bi, i: (bi, i, 0))
    return pl.pallas_call(
        _dsa_prep_body,
        grid=(b, l // tl),
        in_specs=[pl.BlockSpec((1, tl, w2), lambda bi, i: (bi, i, COL_QKV_B // w2)),
                  pl.BlockSpec((1, tl, wi), lambda bi, i: (bi, i, COL_Q_I // wi)),
                  big(LANES), tab, tab, tab, tab],
        out_specs=[big(DSA_WIDTH), big(DSA_WIDTH), big(wi), big(IDX_HEAD_DIM)],
        out_shape=[jax.ShapeDtypeStruct((b, l, DSA_WIDTH), BF16),
                   jax.ShapeDtypeStruct((b, l, DSA_WIDTH), BF16),
                   jax.ShapeDtypeStruct((b, l, wi), BF16),
                   jax.ShapeDtypeStruct((b, l, IDX_HEAD_DIM), BF16)],
        compiler_params=_cparams("parallel", "parallel"),
        name="dsa_prep",
    )(proj, proj, smalls, *tabs)


def _dsa_select_body(qi_ref, sm_ref, ki_ref, bias_in_ref, o_ref, s_ref, *, tq, tc, topk, n_c, tile0):
    del bias_in_ref
    nl = tc // LANES
    half = tq // 2
    q0 = (tile0 + pl.program_id(1)) * tq
    kf = float(topk)
    w_all = sm_ref[0] * (IDX_HEADS ** -0.5)
    rb = min(SELECT_ROW_BLOCK, half)
    rep = lambda col: jnp.broadcast_to(col, (col.shape[0], LANES))
    lane_tiles = lambda x: [x[:, j * LANES:(j + 1) * LANES] for j in range(nl)]

    def score_chunk(k0, carry, last):
        mn, mx, n_pos, n_nonneg = carry
        kc = ki_ref[0, pl.ds(k0, tc), :]
        s = jnp.zeros((tq, tc), F32)
        for h in range(IDX_HEADS):
            qh = qi_ref[0, :, h * IDX_HEAD_DIM:(h + 1) * IDX_HEAD_DIM]
            s = s + w_all[:, SM_W + h:SM_W + h + 1] * jnp.maximum(_dot_nt(qh, kc), 0.0)
        if last:
            qpos = q0 + lax.broadcasted_iota(jnp.int32, (tq, tc), 0)
            causal = k0 + lax.broadcasted_iota(jnp.int32, (tq, tc), 1) <= qpos
            s_lo, s_hi = jnp.where(causal, s, -jnp.inf), jnp.where(causal, s, jnp.inf)
        else:
            s_lo = s_hi = s
        s_ref[:, pl.ds(k0, tc)] = s_lo
        for lo_t, hi_t in zip(lane_tiles(s_lo), lane_tiles(s_hi)):
            mx = jnp.maximum(mx, lo_t)
            mn = jnp.minimum(mn, hi_t)
            n_pos = n_pos + jnp.where(lo_t > 0.0, 1.0, 0.0)
            n_nonneg = n_nonneg + jnp.where(lo_t >= 0.0, 1.0, 0.0)
        return mn, mx, n_pos, n_nonneg

    zeros = jnp.zeros((tq, LANES), F32)
    stats = lax.fori_loop(0, n_c - 1, lambda ci, c: score_chunk(pl.multiple_of(ci * tc, tc), c, False),
                          (zeros + jnp.inf, zeros - jnp.inf, zeros, zeros), unroll=2)
    mn, mx, n_pos, n_nonneg = score_chunk((n_c - 1) * tc, stats, True)
    row_min = rep(jnp.min(mn, axis=1, keepdims=True))
    row_max = rep(jnp.max(mx, axis=1, keepdims=True))
    n_pos = rep(jnp.sum(n_pos, axis=1, keepdims=True))
    n_nonneg = rep(jnp.sum(n_nonneg, axis=1, keepdims=True))

    def reduce_keys(fn, init, combine, r_lo, r_hi, unrolled):
        accs = []
        for r0 in range(r_lo, r_hi, rb):
            def chunk(k0, acc, r0=r0):
                for j in range(nl):
                    acc = combine(acc, fn(s_ref[r0:r0 + rb, pl.ds(k0 + j * LANES, LANES)], r0 - r_lo))
                return acc
            acc = jnp.full((rb, LANES), init, F32)
            if unrolled:
                for ci in range(n_c):
                    acc = chunk(ci * tc, acc)
            else:
                acc = lax.fori_loop(0, n_c, lambda ci, a: chunk(pl.multiple_of(ci * tc, tc), a), acc)
            accs.append(acc)
        return jnp.concatenate(accs, axis=0)

    def count_acc(op, th, r_lo, r_hi, unrolled=True):
        return reduce_keys(lambda t, r: jnp.where(op(t, th[r:r + rb]), 1.0, 0.0), 0.0, jnp.add, r_lo, r_hi, unrolled)

    lane_max = lambda x: rep(jnp.max(x, axis=1, keepdims=True))
    lane_sum = lambda x: rep(jnp.sum(x, axis=1, keepdims=True))
    ge = lambda a, b: a >= b
    count = lambda op, th: lane_sum(count_acc(op, th, 0, tq, unrolled=False))
    is_open = lambda lo, hi: jnp.max(hi - lo) > 0.0
    middle = lambda lo, hi: lo + (hi - lo) * 0.5

    def narrow(lo, hi, mid, cnt):
        hit = cnt == kf
        above = cnt > kf
        return jnp.where(hit | above, mid, lo), jnp.where(hit | jnp.logical_not(above), mid, hi)

    n_causal = (q0 + 1 + lax.broadcasted_iota(jnp.int32, (tq, LANES), 0)).astype(F32)
    few = n_causal <= kf
    zero_tied = (n_pos < kf) & (n_nonneg >= kf)
    starts_closed = few | zero_tied
    closed_at = jnp.where(few, -jnp.finfo(F32).max, 0.0)
    positive = n_pos >= kf
    lo = jnp.where(starts_closed, closed_at, jnp.where(positive, jnp.maximum(row_min, 0.0), row_min))
    hi = jnp.where(starts_closed, closed_at, jnp.where(positive, row_max, jnp.minimum(row_max, 0.0)))
    surplus = jnp.where((zero_tied & (n_nonneg > kf)) | (jnp.logical_not(starts_closed) & (lo >= hi)), 1.0, 0.0)

    def search_cond(c):
        return (c[0] < BISECT_STEPS) & (c[7] > 0)

    def search_body(c):
        it, lo_a, hi_a, lo_b, hi_b, acc_b, mid_b, _ = c
        go_on = (is_open(lo_a, hi_a) | is_open(lo_b, hi_b)).astype(jnp.int32)
        mid_a = middle(lo_a, hi_a)
        acc_a = count_acc(ge, mid_a, 0, half)
        lo_b, hi_b = narrow(lo_b, hi_b, mid_b, lane_sum(acc_b))
        mid_b = middle(lo_b, hi_b)
        acc_b = count_acc(ge, mid_b, half, tq)
        lo_a, hi_a = narrow(lo_a, hi_a, mid_a, lane_sum(acc_a))
        return it + 1, lo_a, hi_a, lo_b, hi_b, acc_b, mid_b, go_on

    lo_a, hi_a, lo_b, hi_b = lo[:half], hi[:half], lo[half:], hi[half:]
    mid_b = middle(lo_b, hi_b)
    _, lo_a, hi_a, lo_b, hi_b, acc_b, mid_b, _ = lax.while_loop(
        search_cond, search_body,
        (jnp.int32(0), lo_a, hi_a, lo_b, hi_b, count_acc(ge, mid_b, half, tq), mid_b, jnp.int32(1)))
    lo_b, hi_b = narrow(lo_b, hi_b, mid_b, lane_sum(acc_b))
    lo = jnp.concatenate([lo_a, lo_b], axis=0)
    hi = jnp.concatenate([hi_a, hi_b], axis=0)

    def peel_cond(c):
        return (c[0] < topk + 2) & (c[4] > 0)

    def peel_body(c):
        it, lo, hi, surplus, _ = c
        still = hi > lo
        widen = jnp.where(it == 0, jnp.finfo(F32).max, 0.0)
        below = jnp.where(hi == row_max, hi + widen, hi)
        top = lane_max(reduce_keys(
            lambda t, r: jnp.where((t >= lo[r:r + rb]) & (t < below[r:r + rb]), t, -jnp.inf),
            -jnp.inf, jnp.maximum, 0, tq, False))
        n_top = count(ge, top)
        found = still & (n_top >= kf)
        lo = jnp.where(found, top, lo)
        hi = jnp.where(still, top, hi)
        surplus = jnp.where(found & (n_top > kf), 1.0, surplus)
        return it + 1, lo, hi, surplus, is_open(lo, hi).astype(jnp.int32)

    _, thr, _, surplus, _ = lax.while_loop(
        peel_cond, peel_body, (jnp.int32(0), lo, hi, surplus, is_open(lo, hi).astype(jnp.int32)))

    for r0 in range(0, tq, rb):
        rows = slice(r0, r0 + rb)
        thr_r = thr[rows]
        has_excess = jnp.max(surplus[rows]) > 0.0

        def write_plain(ci, carry, rows=rows, thr_r=thr_r):
            k0 = pl.multiple_of(ci * tc, tc)
            for j in range(nl):
                cols = pl.ds(k0 + j * LANES, LANES)
                o_ref[0, rows, cols] = jnp.where(s_ref[rows, cols] >= thr_r, 0.0, MASK_BIAS).astype(o_ref.dtype)
            return carry

        @pl.when(jnp.logical_not(has_excess))
        def _(write_plain=write_plain):
            lax.fori_loop(0, n_c, write_plain, 0)

        @pl.when(has_excess)
        def _(r0=r0, rows=rows, thr_r=thr_r):
            n_gt = lane_sum(count_acc(lambda a, b: a > b, thr_r, r0, r0 + rb, unrolled=False))
            need_c = (kf - n_gt)[:, 0:1]
            thr_c = thr_r[:, 0:1]
            r = lax.broadcasted_iota(jnp.int32, (tc, tc), 0)
            cidx = lax.broadcasted_iota(jnp.int32, (tc, tc), 1)
            upper = jnp.where(r < cidx, 1.0, 0.0).astype(BF16)

            def write_ties(ci, seen):
                k0 = pl.multiple_of(ci * tc, tc)
                sc = s_ref[rows, pl.ds(k0, tc)]
                eq = sc == thr_c
                before = seen + _dot(jnp.where(eq, 1.0, 0.0).astype(BF16), upper)
                keep = (sc > thr_c) | (eq & (before < need_c))
                o_ref[0, rows, pl.ds(k0, tc)] = jnp.where(keep, 0.0, MASK_BIAS).astype(o_ref.dtype)
                return seen + jnp.sum(jnp.where(eq, 1.0, 0.0), axis=1, keepdims=True)

            lax.fori_loop(0, n_c, write_ties, jnp.zeros((rb, 1), F32))


def _dsa_select(qi, smalls, ki, bias, *, tq, tc, topk):
    b, l, wi = qi.shape
    per_call = tc // tq
    for n_c in range(1, l // tc + 1):
        tile0 = (n_c - 1) * per_call
        bias = pl.pallas_call(
            functools.partial(_dsa_select_body, tq=tq, tc=tc, topk=topk, n_c=n_c, tile0=tile0),
            grid=(b, per_call),
            in_specs=[pl.BlockSpec((1, tq, wi), lambda bi, t, tile0=tile0: (bi, tile0 + t, 0)),
                      pl.BlockSpec((1, tq, LANES), lambda bi, t, tile0=tile0: (bi, tile0 + t, 0)),
                      pl.BlockSpec((1, n_c * tc, IDX_HEAD_DIM), lambda bi, t: (bi, 0, 0)),
                      pl.BlockSpec(memory_space=pl.ANY)],
            out_specs=pl.BlockSpec((1, tq, n_c * tc), lambda bi, t, tile0=tile0: (bi, tile0 + t, 0)),
            out_shape=jax.ShapeDtypeStruct((b, l, l), BF16),
            scratch_shapes=[pltpu.VMEM((tq, n_c * tc), F32)],
            input_output_aliases={3: 0},
            compiler_params=_cparams("parallel", "parallel"),
            name=f"dsa_select_{n_c}",
        )(qi, smalls, ki, bias)
    return bias


def _dsa_attn_body(qt_ref, kt_ref, q_ref, k_ref, v_ref, b_ref, o_ref, m_ref, acc_ref):
    step = pl.program_id(1)
    qi, ki = qt_ref[step], kt_ref[step]
    d = DSA_HEAD_DIM

    def key_tile(first):
        bias = b_ref[0]
        ones = jnp.ones((v_ref.shape[1], d), BF16)
        for h0 in range(0, DSA_HEADS, ATTN_HEAD_GROUP):
            heads = range(h0, h0 + ATTN_HEAD_GROUP)
            s = [_dot_nt(q_ref[0, :, h * d:(h + 1) * d], k_ref[0, :, h * d:(h + 1) * d]).astype(BF16) + bias
                 for h in heads]
            m_new = [jnp.broadcast_to(jnp.max(x, axis=-1, keepdims=True).astype(F32), m_ref.shape[1:]) for x in s]
            if not first:
                m_old = [m_ref[h] for h in heads]
                m_new = [jnp.maximum(mo, mn) for mo, mn in zip(m_old, m_new)]
                alpha = [jnp.exp2(mo - mn) for mo, mn in zip(m_old, m_new)]
            p = [jnp.exp2(x - mn[:, 0:1].astype(BF16)) for x, mn in zip(s, m_new)]
            upd = [_dot(x, jnp.concatenate([v_ref[0, :, h * d:(h + 1) * d], ones], axis=1))
                   for x, h in zip(p, heads)]
            for i, h in enumerate(heads):
                if first:
                    acc_ref[h] = upd[i]
                else:
                    acc_ref[h, :, :d] = alpha[i] * acc_ref[h, :, :d] + upd[i][:, :d]
                    acc_ref[h, :, d:] = alpha[i] * acc_ref[h, :, d:] + upd[i][:, d:]
                m_ref[h] = m_new[i]

    pl.when(ki == 0)(lambda: key_tile(True))
    pl.when(ki > 0)(lambda: key_tile(False))

    @pl.when(ki == qi)
    def _():
        for h in range(DSA_HEADS):
            lo = h * d
            acc = acc_ref[h]
            o_ref[0, :, lo:lo + d] = (acc[:, :d] / acc[:, d:]).astype(o_ref.dtype)


def _dsa_attention(qb, kb, proj, bias, *, t):
    b, l, w = qb.shape
    n = l // t
    pairs = [(i, j) for i in range(n) for j in range(i + 1)]
    q_tile = jnp.asarray([p[0] for p in pairs], jnp.int32)
    k_tile = jnp.asarray([p[1] for p in pairs], jnp.int32)
    v_col = (COL_QKV_B + 2 * DSA_WIDTH) // DSA_WIDTH
    q_idx = lambda bi, s, qt, kt: (bi, qt[s], 0)
    kv_idx = lambda bi, s, qt, kt: (bi, kt[s], 0)
    return pl.pallas_call(
        _dsa_attn_body,
        grid_spec=pltpu.PrefetchScalarGridSpec(
            num_scalar_prefetch=2,
            grid=(b, len(pairs)),
            in_specs=[pl.BlockSpec((1, t, w), q_idx),
                      pl.BlockSpec((1, t, w), kv_idx),
                      pl.BlockSpec((1, t, w), lambda bi, s, qt, kt: (bi, kt[s], v_col)),
                      pl.BlockSpec((1, t, t), lambda bi, s, qt, kt: (bi, qt[s], kt[s]))],
            out_specs=pl.BlockSpec((1, t, w), q_idx),
            scratch_shapes=[pltpu.VMEM((DSA_HEADS, t, LANES), F32),
                            pltpu.VMEM((DSA_HEADS, t, 2 * DSA_HEAD_DIM), F32)]),
        out_shape=jax.ShapeDtypeStruct((b, l, w), BF16),
        compiler_params=_cparams("parallel", "arbitrary"),
        name="dsa_attention",
    )(q_tile, k_tile, qb, kb, proj, bias)


def _rope_tables(seq_len):
    def tab(dim):
        inv_freq = 1.0 / (ROPE_THETA ** (jnp.arange(0, dim, 2, dtype=F32) / dim))
        split = math.gcd(seq_len, ROPE_SPLIT)
        ang_lo = jnp.arange(split, dtype=F32)[None, :, None] * inv_freq
        ang_hi = (jnp.arange(seq_len // split, dtype=F32) * split)[:, None, None] * inv_freq
        c = (jnp.cos(ang_hi) * jnp.cos(ang_lo) - jnp.sin(ang_hi) * jnp.sin(ang_lo)).reshape(seq_len, dim // 2)
        s = (jnp.sin(ang_hi) * jnp.cos(ang_lo) + jnp.cos(ang_hi) * jnp.sin(ang_lo)).reshape(seq_len, dim // 2)
        return jnp.concatenate([c, c], axis=1), jnp.concatenate([-s, s], axis=1)
    ca, sa = tab(DSA_HEAD_DIM)
    c64, s64 = tab(IDX_HEAD_DIM)
    ci = jnp.concatenate([c64, c64], axis=1)
    si = jnp.concatenate([s64, s64], axis=1)
    return ca, sa, ci, si


def _regroup_w_in(w):
    sizes = (3 * GDN_WIDTH, GDN_WIDTH, GDN_HEADS, GDN_HEADS, 3 * DSA_WIDTH,
             IDX_HEADS * IDX_HEAD_DIM, IDX_HEAD_DIM, IDX_HEADS, D_MODEL, D_MODEL)
    parts, start = [], 0
    for s in sizes:
        parts.append(w[..., start:start + s])
        start += s
    qkv_a, z_a, a_a, b_a, qkv_b, q_i, k_i, w_i, g_a, g_b = parts
    main = jnp.concatenate([qkv_a, z_a, qkv_b, g_a, g_b, q_i], axis=-1).astype(BF16)
    pad = jnp.zeros(w.shape[:-1] + (SM_K - SM_W - IDX_HEADS,), w.dtype)
    smalls = jnp.concatenate([a_a, b_a, w_i, pad, k_i], axis=-1)
    return main, smalls


def kernel(x, norm_mix, w_in, conv_w, a_log, dt_bias, gdn_norm, w_out_gdn, w_out_dsa, w_o,
           norm_ffn, w_gate_up, w_down, norm_final):
    b, l, dm = x.shape
    depth = w_in.shape[0]
    m = b * l
    assert dm == D_MODEL and DSA_HEAD_DIM == LANES and GDN_HEAD_DIM == LANES
    topk = min(TOPK_MAX, l // 4)
    tabs = _rope_tables(l)
    rows, seq = (lambda t: min(t, m)), (lambda t: min(t, l))
    h = x.reshape(m, dm)
    bias = jnp.full((b, l, l), MASK_BIAS, BF16)
    w_main, w_smalls = _regroup_w_in(w_in)
    w_out_gdn, w_out_dsa, w_o, w_gate_up, w_down = (
        w.astype(BF16) for w in (w_out_gdn, w_out_dsa, w_o, w_gate_up, w_down))
    for layer in range(depth):
        proj = _norm_matmul(h, norm_mix[layer], w_main, layer, tm=rows(ROW_TILE),
                            tn=MAIN_COLS // PROJ_COL_TILES, out_dtype=BF16)
        smalls = _norm_matmul(h, norm_mix[layer], w_smalls, layer, tm=rows(SMALLS_ROW_TILE), tn=LANES,
                              out_dtype=F32, exact=True)
        proj = proj.reshape(b, l, MAIN_COLS)
        smalls = smalls.reshape(b, l, LANES)

        q_a, k_a, v_a = _gdn_prep(proj, conv_w[layer], tl=seq(CONV_TIME_BLOCK))
        o_a = _gdn_chunk(q_a, k_a, v_a, proj, smalls, a_log[layer], dt_bias[layer], gdn_norm[layer],
                         tl=seq(GDN_TIME_BLOCK))

        q_b, k_b, q_i, k_i = _dsa_prep(proj, smalls, tabs, tl=seq(ROPE_TIME_BLOCK))
        bias = _dsa_select(q_i, smalls, k_i, bias, tq=seq(SELECT_Q_TILE), tc=seq(SELECT_KEY_CHUNK), topk=topk)
        o_b = _dsa_attention(q_b, k_b, proj, bias, t=seq(ATTN_TILE))

        h = _mixer_out(o_a.reshape(m, GDN_WIDTH), o_b.reshape(m, DSA_WIDTH), w_out_gdn, w_out_dsa, w_o, layer,
                       proj.reshape(m, MAIN_COLS), h, tm=rows(ROW_TILE))
        last = layer == depth - 1
        h = _ffn(h, norm_ffn[layer], w_gate_up, w_down, layer,
                 tm=rows(ROW_TILE), final_gain=norm_final if last else None)
    return h.reshape(b, l, dm)
```

```python
import functools
import math

import jax
import jax.numpy as jnp
from jax import lax
from jax.experimental import pallas as pl
from jax.experimental.pallas import tpu as pltpu

F32 = jnp.float32
BF16 = jnp.bfloat16

D_MODEL = 1024
GDN_HEADS = 8
GDN_HEAD_DIM = 128
GDN_WIDTH = GDN_HEADS * GDN_HEAD_DIM
CONV_K = 4
GDN_CHUNK = 64
DSA_HEADS = 8
DSA_HEAD_DIM = 128
DSA_WIDTH = DSA_HEADS * DSA_HEAD_DIM
IDX_HEADS = 8
IDX_HEAD_DIM = 64
TOPK_MAX = 256
ROPE_THETA = 10000.0
ROPE_SPLIT = 64
NORM_EPS = 1e-6

LANES = 128
VMEM_LIMIT = 48 * 1024 * 1024
ROW_TILE = 512
SMALLS_ROW_TILE = 1024
CONV_TIME_BLOCK = 256
GDN_TIME_BLOCK = 512
ROPE_TIME_BLOCK = 512
SELECT_Q_TILE = 256
SELECT_KEY_CHUNK = 512
ATTN_TILE = 512
PROJ_COL_TILES = 4
REGROUP_ROW_TILE = 256
BISECT_STEPS = 32
SELECT_ROW_BLOCK = 64
ATTN_HEAD_GROUP = 4
MASK_BIAS = -(2.0 ** 100)
Q_SCALE = DSA_HEAD_DIM ** -0.5 * math.log2(math.e)

COL_QKV_A = 0
COL_Z_A = 3 * GDN_WIDTH
COL_QKV_B = COL_Z_A + GDN_WIDTH
COL_G_A = COL_QKV_B + 3 * DSA_WIDTH
COL_Q_I = COL_G_A + 2 * D_MODEL
MAIN_COLS = COL_Q_I + IDX_HEADS * IDX_HEAD_DIM
SM_A, SM_B, SM_W, SM_K = 0, 8, 16, 64


def _cparams(*sem):
    return pltpu.CompilerParams(dimension_semantics=sem, vmem_limit_bytes=VMEM_LIMIT)


def _rmsnorm_rows(x, gain):
    ms = jnp.mean(x * x, axis=-1, keepdims=True)
    return x * lax.rsqrt(ms + NORM_EPS) * gain


def _dot(a, b):
    return jnp.dot(a, b, preferred_element_type=F32)


def _dot_nt(a, b):
    return lax.dot_general(a, b, (((1,), (1,)), ((), ())), preferred_element_type=F32)


def _resident(shape, layer=None):
    if layer is None:
        return pl.BlockSpec(shape, lambda i, j: (0,) * len(shape), pipeline_mode=pl.Buffered(1))
    return pl.BlockSpec((None,) + shape, lambda i, j: (layer,) + (0,) * len(shape), pipeline_mode=pl.Buffered(1))


def _norm_mm_body(x_ref, g_ref, w_ref, o_ref, xn_ref, *, tn, exact):
    @pl.when(pl.program_id(1) == 0)
    def _():
        xn_ref[...] = _rmsnorm_rows(x_ref[...], g_ref[...]).astype(xn_ref.dtype)

    w = w_ref[:, pl.ds(pl.multiple_of(pl.program_id(1) * tn, tn), tn)]
    if exact:
        x = xn_ref[...]
        x_hi, w_hi = x.astype(BF16), w.astype(BF16)
        x_lo = (x - x_hi.astype(F32)).astype(BF16)
        w_lo = (w - w_hi.astype(F32)).astype(BF16)
        o = _dot(x_hi, w_hi) + (_dot(x_hi, w_lo) + _dot(x_lo, w_hi))
    else:
        o = _dot(xn_ref[...], w)
    o_ref[...] = o.astype(o_ref.dtype)


def _norm_matmul(x, gain, w, layer, *, tm, tn, out_dtype, exact=False):
    m, k = x.shape
    n = w.shape[2]
    return pl.pallas_call(
        functools.partial(_norm_mm_body, tn=tn, exact=exact),
        grid=(m // tm, n // tn),
        in_specs=[pl.BlockSpec((tm, k), lambda i, j: (i, 0)),
                  _resident((1, k)),
                  _resident((k, n), layer)],
        out_specs=pl.BlockSpec((tm, tn), lambda i, j: (i, j)),
        out_shape=jax.ShapeDtypeStruct((m, n), out_dtype),
        scratch_shapes=[pltpu.VMEM((tm, k), w.dtype)],
        compiler_params=_cparams("parallel", "arbitrary"),
        name="norm_matmul_exact" if exact else "norm_matmul",
    )(x, gain.reshape(1, k), w)


def _resident1(shape, layer=None):
    if layer is None:
        return pl.BlockSpec(shape, lambda i: (0,) * len(shape), pipeline_mode=pl.Buffered(1))
    return pl.BlockSpec((None,) + shape, lambda i: (layer,) + (0,) * len(shape), pipeline_mode=pl.Buffered(1))


def _ffn_body(x_ref, g_ref, wgu_ref, wd_ref, *rest, f, final_norm):
    if final_norm:
        gf_ref, o_ref = rest
    else:
        (o_ref,) = rest
    x = x_ref[...]
    xn = _rmsnorm_rows(x, g_ref[...]).astype(BF16)
    g = _dot(xn, wgu_ref[:, :f])
    u = _dot(xn, wgu_ref[:, f:])
    act = (g * jax.nn.sigmoid(g) * u).astype(BF16)
    h = x + _dot(act, wd_ref[...])
    if final_norm:
        h = _rmsnorm_rows(h, gf_ref[...])
    o_ref[...] = h


def _ffn(x, gain, w_gate_up, w_down, layer, *, tm, final_gain=None):
    m, k = x.shape
    f = w_down.shape[1]
    final_norm = final_gain is not None
    in_specs = [pl.BlockSpec((tm, k), lambda i: (i, 0)),
                _resident1((1, k)), _resident1((k, 2 * f), layer), _resident1((f, k), layer)]
    args = [x, gain.reshape(1, k), w_gate_up, w_down]
    if final_norm:
        in_specs.append(_resident1((1, k)))
        args.append(final_gain.reshape(1, k))
    return pl.pallas_call(
        functools.partial(_ffn_body, f=f, final_norm=final_norm),
        grid=(m // tm,),
        in_specs=in_specs,
        out_specs=pl.BlockSpec((tm, k), lambda i: (i, 0)),
        out_shape=jax.ShapeDtypeStruct((m, k), F32),
        compiler_params=_cparams("parallel"),
        name="ffn_norm" if final_norm else "ffn",
    )(*args)


def _mixer_out_body(oa_ref, ob_ref, wa_ref, wb_ref, wo_ref, ga_ref, gb_ref, h_ref, o_ref):
    ya = _dot(oa_ref[...], wa_ref[...])
    yb = _dot(ob_ref[...], wb_ref[...])
    ga = jax.nn.sigmoid(ga_ref[...].astype(F32))
    gb = jax.nn.sigmoid(gb_ref[...].astype(F32))
    merged = (ga * ya + gb * yb).astype(BF16)
    o_ref[...] = h_ref[...] + _dot(merged, wo_ref[...])


def _mixer_out(oa, ob, wa, wb, wo, layer, proj, h, *, tm):
    m, k = oa.shape
    n = wa.shape[2]
    row = lambda w: pl.BlockSpec((tm, w), lambda i: (i, 0))
    return pl.pallas_call(
        _mixer_out_body,
        grid=(m // tm,),
        in_specs=[row(k), row(k), _resident1((k, n), layer), _resident1((k, n), layer), _resident1((n, n), layer),
                  pl.BlockSpec((tm, n), lambda i: (i, COL_G_A // n)),
                  pl.BlockSpec((tm, n), lambda i: (i, COL_G_A // n + 1)),
                  row(n)],
        out_specs=row(n),
        out_shape=jax.ShapeDtypeStruct((m, n), F32),
        compiler_params=_cparams("parallel"),
        name="mixer_out",
    )(oa, ob, wa, wb, wo, proj, proj, h)


HALO = 16
CONV_ROWS = 128


def _gdn_prep_body(x_ref, halo_ref, cw_ref, q_ref, k_ref, v_ref, xs_ref, *, tl):
    i = pl.program_id(1)
    sub = CONV_ROWS
    xs_ref[0:HALO, :] = jnp.where(i == 0, jnp.zeros_like(halo_ref[0]), halo_ref[0])
    xs_ref[HALO:HALO + tl, :] = x_ref[0]
    t = lax.broadcasted_iota(jnp.int32, ((CONV_K - 1) * sub, HALO + sub), 0)
    c = lax.broadcasted_iota(jnp.int32, ((CONV_K - 1) * sub, HALO + sub), 1)
    shift = jnp.where(c == t % sub + HALO - (CONV_K - 1) + t // sub, 1.0, 0.0).astype(BF16)
    for r0 in range(0, tl, sub):
        win = xs_ref[r0:r0 + HALO + sub, :]
        z = _dot(shift, win)
        y = cw_ref[CONV_K - 1:CONV_K, :] * win[HALO:, :].astype(F32)
        for j in range(CONV_K - 1):
            y = y + cw_ref[j:j + 1, :] * z[j * sub:(j + 1) * sub]
        y = y * jax.nn.sigmoid(y)
        rows = slice(r0, r0 + sub)
        for h in range(GDN_HEADS):
            lo = h * GDN_HEAD_DIM
            qh = y[:, lo:lo + GDN_HEAD_DIM]
            kh = y[:, GDN_WIDTH + lo:GDN_WIDTH + lo + GDN_HEAD_DIM]
            qn = qh * lax.rsqrt(jnp.sum(qh * qh, axis=-1, keepdims=True) + NORM_EPS)
            kn = kh * lax.rsqrt(jnp.sum(kh * kh, axis=-1, keepdims=True) + NORM_EPS)
            q_ref[0, rows, lo:lo + GDN_HEAD_DIM] = (qn * (GDN_HEAD_DIM ** -0.5)).astype(q_ref.dtype)
            k_ref[0, rows, lo:lo + GDN_HEAD_DIM] = kn.astype(k_ref.dtype)
        v_ref[0, rows, :] = y[:, 2 * GDN_WIDTH:].astype(v_ref.dtype)


def _gdn_prep(proj, conv_w, *, tl):
    b, l, _ = proj.shape
    c = 3 * GDN_WIDTH
    r = tl // HALO
    out = jax.ShapeDtypeStruct((b, l, GDN_WIDTH), BF16)
    ospec = pl.BlockSpec((1, tl, GDN_WIDTH), lambda bi, i: (bi, i, 0))
    return pl.pallas_call(
        functools.partial(_gdn_prep_body, tl=tl),
        grid=(b, l // tl),
        in_specs=[pl.BlockSpec((1, tl, c), lambda bi, i: (bi, i, 0)),
                  pl.BlockSpec((1, HALO, c), lambda bi, i: (bi, jnp.maximum(i * r - 1, 0), 0)),
                  pl.BlockSpec((CONV_K, c), lambda bi, i: (0, 0))],
        out_specs=[ospec, ospec, ospec],
        out_shape=[out, out, out],
        scratch_shapes=[pltpu.VMEM((HALO + tl, c), BF16)],
        compiler_params=_cparams("parallel", "parallel"),
        name="gdn_prep",
    )(proj, proj, conv_w)


def _gdn_chunk_body(q_ref, k_ref, v_ref, z_ref, sm_ref, alog_ref, dtb_ref, gn_ref, o_ref,
                    s_ref, wq_ref, qkk_ref, u_ref, gl_ref, *, nch, nb):
    c = GDN_CHUNK
    d = GDN_HEAD_DIM
    probs = [(bb, h) for bb in range(nb) for h in range(GDN_HEADS)]

    @pl.when(pl.program_id(1) == 0)
    def _():
        s_ref[...] = jnp.zeros_like(s_ref)

    row = lax.broadcasted_iota(jnp.int32, (c, c), 0)
    col = lax.broadcasted_iota(jnp.int32, (c, c), 1)
    tril_incl = row >= col
    tril_strict = row > col
    eye = (row == col).astype(F32)
    row_id = lax.broadcasted_iota(jnp.int32, (c, LANES), 0)

    def intra(ci, carry):
        r0 = pl.multiple_of(ci * c, c)
        gam_col, gam_row, beta_all = [], [], []
        for bb in range(nb):
            sm = sm_ref[bb, pl.ds(r0, c), :]
            g_all = -jnp.exp(alog_ref[...]) * jax.nn.softplus(sm + dtb_ref[...])
            beta_all.append(jax.nn.sigmoid(sm))
            gam = g_all
            step = 1
            while step < c:
                gam = gam + jnp.where(row_id >= step, pltpu.roll(gam, step, axis=0), 0.0)
                step *= 2
            gam_col.append(gam)
            gam_row.append(gam.T)
        gc = [gam_col[bb][:, SM_A + h:SM_A + h + 1] for bb, h in probs]
        bc = [beta_all[bb][:, SM_B + h:SM_B + h + 1] for bb, h in probs]
        g_last = [g[c - 1:c, :] for g in gc]
        decay = [jnp.exp(jnp.minimum(g - gam_row[bb][SM_A + h:SM_A + h + 1, :], 0.0))
                 for g, (bb, h) in zip(gc, probs)]
        qh = [q_ref[bb, pl.ds(r0, c), h * d:(h + 1) * d] for bb, h in probs]
        kh = [k_ref[bb, pl.ds(r0, c), h * d:(h + 1) * d] for bb, h in probs]
        vh = [v_ref[bb, pl.ds(r0, c), h * d:(h + 1) * d] for bb, h in probs]
        kq = [_dot_nt(jnp.concatenate([k, q], axis=0), k) for k, q in zip(kh, qh)]
        qk = [jnp.where(tril_incl, x[c:] * dc, 0.0) for x, dc in zip(kq, decay)]
        n_pow = [-jnp.where(tril_strict, b_ * x[:c] * dc, 0.0) for x, b_, dc in zip(kq, bc, decay)]
        t_inv = [eye + n for n in n_pow]
        n_pow = [_dot(n.astype(BF16), n.astype(BF16)) for n in n_pow]
        for _ in range(int(math.log2(c)) - 2):
            prod = [_dot(jnp.concatenate([n, t], axis=0).astype(BF16), n.astype(BF16))
                    for n, t in zip(n_pow, t_inv)]
            n_pow = [p[:c] for p in prod]
            t_inv = [t + p[c:] for t, p in zip(t_inv, prod)]
        t_inv = [t + _dot(t.astype(BF16), n.astype(BF16)) for n, t in zip(n_pow, t_inv)]
        e_gc = [jnp.exp(g) for g in gc]
        kf = [k.astype(F32) for k in kh]
        rhs = [jnp.concatenate([v.astype(F32) * b_, k * (b_ * e)], axis=1)
               for v, k, b_, e in zip(vh, kf, bc, e_gc)]
        uw = [_dot(t.astype(BF16), r.astype(BF16)) for t, r in zip(t_inv, rhs)]
        for i, (bb, h) in enumerate(probs):
            k_dec = kf[i] * jnp.exp(g_last[i] - gc[i])
            u_ref[bb, ci, h] = uw[i][:, :d]
            wq_ref[bb, ci, h] = jnp.concatenate([uw[i][:, d:], qh[i].astype(F32) * e_gc[i]], axis=0).astype(BF16)
            qkk_ref[bb, ci, h] = jnp.concatenate([qk[i], k_dec.T], axis=0).astype(BF16)
            gl_ref[bb, ci, h] = jnp.broadcast_to(jnp.exp(g_last[i]), (1, d))
        return carry

    lax.fori_loop(0, nch, intra, 0)

    def scan(ci, carry):
        r0 = pl.multiple_of(ci * c, c)
        s_old = [s_ref[bb, h] for bb, h in probs]
        m1 = [_dot(wq_ref[bb, ci, h], s.astype(BF16)) for s, (bb, h) in zip(s_old, probs)]
        v_new = [u_ref[bb, ci, h] - m[:c] for m, (bb, h) in zip(m1, probs)]
        m2 = [_dot(qkk_ref[bb, ci, h], v.astype(BF16)) for v, (bb, h) in zip(v_new, probs)]
        for i, (bb, h) in enumerate(probs):
            s_ref[bb, h] = s_old[i] * gl_ref[bb, ci, h] + m2[i][c:]
            o = m1[i][c:] + m2[i][:c]
            o = o * lax.rsqrt(jnp.mean(o * o, axis=-1, keepdims=True) + NORM_EPS) * gn_ref[...]
            z = z_ref[bb, pl.ds(r0, c), h * d:(h + 1) * d].astype(F32)
            o_ref[bb, pl.ds(r0, c), h * d:(h + 1) * d] = (o * (z * jax.nn.sigmoid(z))).astype(o_ref.dtype)
        return carry

    lax.fori_loop(0, nch, scan, 0)


def _gdn_chunk(q, k, v, proj, smalls, a_log, dt_bias, gdn_norm, *, tl):
    b, l, _ = q.shape
    c, d, nh = GDN_CHUNK, GDN_HEAD_DIM, GDN_HEADS
    nch = tl // c
    nb = 2 if b % 2 == 0 else 1
    pad = lambda t, at: jnp.zeros((1, LANES), F32).at[0, at:at + t.shape[0]].set(t.astype(F32))
    qspec = pl.BlockSpec((nb, tl, GDN_WIDTH), lambda bi, i: (bi, i, 0))
    vec = pl.BlockSpec((1, LANES), lambda bi, i: (0, 0))
    return pl.pallas_call(
        functools.partial(_gdn_chunk_body, nch=nch, nb=nb),
        grid=(b // nb, l // tl),
        in_specs=[qspec, qspec, qspec,
                  pl.BlockSpec((nb, tl, GDN_WIDTH), lambda bi, i: (bi, i, COL_Z_A // GDN_WIDTH)),
                  pl.BlockSpec((nb, tl, LANES), lambda bi, i: (bi, i, 0)),
                  vec, vec, vec],
        out_specs=qspec,
        out_shape=jax.ShapeDtypeStruct((b, l, GDN_WIDTH), BF16),
        scratch_shapes=[pltpu.VMEM((nb, nh, d, d), F32),
                        pltpu.VMEM((nb, nch, nh, 2 * c, d), BF16),
                        pltpu.VMEM((nb, nch, nh, c + d, c), BF16),
                        pltpu.VMEM((nb, nch, nh, c, d), F32),
                        pltpu.VMEM((nb, nch, nh, 1, d), F32)],
        compiler_params=_cparams("parallel", "arbitrary"),
        name="gdn_chunk",
    )(q, k, v, proj, smalls, pad(a_log, SM_A), pad(dt_bias, SM_A), gdn_norm.reshape(1, LANES).astype(F32))


def _swap_halves(x, half):
    if 2 * half == LANES:
        return pltpu.roll(x, half, axis=1)
    lane = lax.broadcasted_iota(jnp.int32, x.shape, 1)
    first = (lane % (2 * half)) < half
    return jnp.where(first, pltpu.roll(x, LANES - half, axis=1), pltpu.roll(x, half, axis=1))


def _dsa_prep_body(qk_ref, qi_ref, sm_ref, ca_ref, sa_ref, ci_ref, si_ref,
                   qb_ref, kb_ref, qio_ref, kio_ref):
    ca, sa = ca_ref[...], sa_ref[...]
    for h in range(2 * DSA_HEADS):
        lo = h * DSA_HEAD_DIM
        x = qk_ref[0, :, lo:lo + DSA_HEAD_DIM].astype(F32)
        y = x * ca + _swap_halves(x, DSA_HEAD_DIM // 2) * sa
        if h < DSA_HEADS:
            qb_ref[0, :, lo:lo + DSA_HEAD_DIM] = (y * Q_SCALE).astype(BF16)
        else:
            lo -= DSA_WIDTH
            kb_ref[0, :, lo:lo + DSA_HEAD_DIM] = y.astype(BF16)
    ci, si = ci_ref[...], si_ref[...]
    for p in range(IDX_HEADS * IDX_HEAD_DIM // LANES):
        lo = p * LANES
        x = qi_ref[0, :, lo:lo + LANES].astype(F32)
        y = x * ci + _swap_halves(x, IDX_HEAD_DIM // 2) * si
        qio_ref[0, :, lo:lo + LANES] = y.astype(BF16)
    x = sm_ref[0]
    y = x * ci + _swap_halves(x, IDX_HEAD_DIM // 2) * si
    kio_ref[0] = (y[:, SM_K:] * (IDX_HEAD_DIM ** -0.5)).astype(BF16)


def _dsa_prep(proj, smalls, tabs, *, tl):
    b, l, _ = proj.shape
    w2 = 2 * DSA_WIDTH
    wi = IDX_HEADS * IDX_HEAD_DIM
    tab = pl.BlockSpec((tl, LANES), lambda bi, i: (i, 0))
    big = lambda n: pl.BlockSpec((1, tl, n), lambda bi, i: (bi, i, 0))
    return pl.pallas_call(
        _dsa_prep_body,
        grid=(b, l // tl),
        in_specs=[pl.BlockSpec((1, tl, w2), lambda bi, i: (bi, i, COL_QKV_B // w2)),
                  pl.BlockSpec((1, tl, wi), lambda bi, i: (bi, i, COL_Q_I // wi)),
                  big(LANES), tab, tab, tab, tab],
        out_specs=[big(DSA_WIDTH), big(DSA_WIDTH), big(wi), big(IDX_HEAD_DIM)],
        out_shape=[jax.ShapeDtypeStruct((b, l, DSA_WIDTH), BF16),
                   jax.ShapeDtypeStruct((b, l, DSA_WIDTH), BF16),
                   jax.ShapeDtypeStruct((b, l, wi), BF16),
                   jax.ShapeDtypeStruct((b, l, IDX_HEAD_DIM), BF16)],
        compiler_params=_cparams("parallel", "parallel"),
        name="dsa_prep",
    )(proj, proj, smalls, *tabs)


def _dsa_select_body(qi_ref, sm_ref, ki_ref, bias_in_ref, o_ref, s_ref, *, tq, tc, topk, n_c, tile0):
    del bias_in_ref
    nl = tc // LANES
    half = tq // 2
    q0 = (tile0 + pl.program_id(1)) * tq
    kf = float(topk)
    w_all = sm_ref[0] * (IDX_HEADS ** -0.5)
    rb = min(SELECT_ROW_BLOCK, half)
    rep = lambda col: jnp.broadcast_to(col, (col.shape[0], LANES))
    lane_tiles = lambda x: [x[:, j * LANES:(j + 1) * LANES] for j in range(nl)]

    def score_chunk(k0, carry, last):
        mn, mx, n_pos, n_nonneg = carry
        kc = ki_ref[0, pl.ds(k0, tc), :]
        s = jnp.zeros((tq, tc), F32)
        for h in range(IDX_HEADS):
            qh = qi_ref[0, :, h * IDX_HEAD_DIM:(h + 1) * IDX_HEAD_DIM]
            s = s + w_all[:, SM_W + h:SM_W + h + 1] * jnp.maximum(_dot_nt(qh, kc), 0.0)
        if last:
            qpos = q0 + lax.broadcasted_iota(jnp.int32, (tq, tc), 0)
            causal = k0 + lax.broadcasted_iota(jnp.int32, (tq, tc), 1) <= qpos
            s_lo, s_hi = jnp.where(causal, s, -jnp.inf), jnp.where(causal, s, jnp.inf)
        else:
            s_lo = s_hi = s
        s_ref[:, pl.ds(k0, tc)] = s_lo
        for lo_t, hi_t in zip(lane_tiles(s_lo), lane_tiles(s_hi)):
            mx = jnp.maximum(mx, lo_t)
            mn = jnp.minimum(mn, hi_t)
            n_pos = n_pos + jnp.where(lo_t > 0.0, 1.0, 0.0)
            n_nonneg = n_nonneg + jnp.where(lo_t >= 0.0, 1.0, 0.0)
        return mn, mx, n_pos, n_nonneg

    zeros = jnp.zeros((tq, LANES), F32)
    stats = lax.fori_loop(0, n_c - 1, lambda ci, c: score_chunk(pl.multiple_of(ci * tc, tc), c, False),
                          (zeros + jnp.inf, zeros - jnp.inf, zeros, zeros), unroll=2)
    mn, mx, n_pos, n_nonneg = score_chunk((n_c - 1) * tc, stats, True)
    row_min = rep(jnp.min(mn, axis=1, keepdims=True))
    row_max = rep(jnp.max(mx, axis=1, keepdims=True))
    n_pos = rep(jnp.sum(n_pos, axis=1, keepdims=True))
    n_nonneg = rep(jnp.sum(n_nonneg, axis=1, keepdims=True))

    def reduce_keys(fn, init, combine, r_lo, r_hi, unrolled):
        accs = []
        for r0 in range(r_lo, r_hi, rb):
            def chunk(k0, acc, r0=r0):
                for j in range(nl):
                    acc = combine(acc, fn(s_ref[r0:r0 + rb, pl.ds(k0 + j * LANES, LANES)], r0 - r_lo))
                return acc
            acc = jnp.full((rb, LANES), init, F32)
            if unrolled:
                for ci in range(n_c):
                    acc = chunk(ci * tc, acc)
            else:
                acc = lax.fori_loop(0, n_c, lambda ci, a: chunk(pl.multiple_of(ci * tc, tc), a), acc)
            accs.append(acc)
        return jnp.concatenate(accs, axis=0)

    def count_acc(op, th, r_lo, r_hi, unrolled=True):
        return reduce_keys(lambda t, r: jnp.where(op(t, th[r:r + rb]), 1.0, 0.0), 0.0, jnp.add, r_lo, r_hi, unrolled)

    lane_max = lambda x: rep(jnp.max(x, axis=1, keepdims=True))
    lane_sum = lambda x: rep(jnp.sum(x, axis=1, keepdims=True))
    ge = lambda a, b: a >= b
    count = lambda op, th: lane_sum(count_acc(op, th, 0, tq, unrolled=False))
    is_open = lambda lo, hi: jnp.max(hi - lo) > 0.0
    middle = lambda lo, hi: lo + (hi - lo) * 0.5

    def narrow(lo, hi, mid, cnt):
        hit = cnt == kf
        above = cnt > kf
        return jnp.where(hit | above, mid, lo), jnp.where(hit | jnp.logical_not(above), mid, hi)

    n_causal = (q0 + 1 + lax.broadcasted_iota(jnp.int32, (tq, LANES), 0)).astype(F32)
    few = n_causal <= kf
    zero_tied = (n_pos < kf) & (n_nonneg >= kf)
    starts_closed = few | zero_tied
    closed_at = jnp.where(few, -jnp.finfo(F32).max, 0.0)
    positive = n_pos >= kf
    lo = jnp.where(starts_closed, closed_at, jnp.where(positive, jnp.maximum(row_min, 0.0), row_min))
    hi = jnp.where(starts_closed, closed_at, jnp.where(positive, row_max, jnp.minimum(row_max, 0.0)))
    surplus = jnp.where((zero_tied & (n_nonneg > kf)) | (jnp.logical_not(starts_closed) & (lo >= hi)), 1.0, 0.0)

    def search_cond(c):
        return (c[0] < BISECT_STEPS) & (c[7] > 0)

    def search_body(c):
        it, lo_a, hi_a, lo_b, hi_b, acc_b, mid_b, _ = c
        go_on = (is_open(lo_a, hi_a) | is_open(lo_b, hi_b)).astype(jnp.int32)
        mid_a = middle(lo_a, hi_a)
        acc_a = count_acc(ge, mid_a, 0, half)
        lo_b, hi_b = narrow(lo_b, hi_b, mid_b, lane_sum(acc_b))
        mid_b = middle(lo_b, hi_b)
        acc_b = count_acc(ge, mid_b, half, tq)
        lo_a, hi_a = narrow(lo_a, hi_a, mid_a, lane_sum(acc_a))
        return it + 1, lo_a, hi_a, lo_b, hi_b, acc_b, mid_b, go_on

    lo_a, hi_a, lo_b, hi_b = lo[:half], hi[:half], lo[half:], hi[half:]
    mid_b = middle(lo_b, hi_b)
    _, lo_a, hi_a, lo_b, hi_b, acc_b, mid_b, _ = lax.while_loop(
        search_cond, search_body,
        (jnp.int32(0), lo_a, hi_a, lo_b, hi_b, count_acc(ge, mid_b, half, tq), mid_b, jnp.int32(1)))
    lo_b, hi_b = narrow(lo_b, hi_b, mid_b, lane_sum(acc_b))
    lo = jnp.concatenate([lo_a, lo_b], axis=0)
    hi = jnp.concatenate([hi_a, hi_b], axis=0)

    def peel_cond(c):
        return (c[0] < topk + 2) & (c[4] > 0)

    def peel_body(c):
        it, lo, hi, surplus, _ = c
        still = hi > lo
        widen = jnp.where(it == 0, jnp.finfo(F32).max, 0.0)
        below = jnp.where(hi == row_max, hi + widen, hi)
        top = lane_max(reduce_keys(
            lambda t, r: jnp.where((t >= lo[r:r + rb]) & (t < below[r:r + rb]), t, -jnp.inf),
            -jnp.inf, jnp.maximum, 0, tq, False))
        n_top = count(ge, top)
        found = still & (n_top >= kf)
        lo = jnp.where(found, top, lo)
        hi = jnp.where(still, top, hi)
        surplus = jnp.where(found & (n_top > kf), 1.0, surplus)
        return it + 1, lo, hi, surplus, is_open(lo, hi).astype(jnp.int32)

    _, thr, _, surplus, _ = lax.while_loop(
        peel_cond, peel_body, (jnp.int32(0), lo, hi, surplus, is_open(lo, hi).astype(jnp.int32)))

    for r0 in range(0, tq, rb):
        rows = slice(r0, r0 + rb)
        thr_r = thr[rows]
        has_excess = jnp.max(surplus[rows]) > 0.0

        def write_plain(ci, carry, rows=rows, thr_r=thr_r):
            k0 = pl.multiple_of(ci * tc, tc)
            for j in range(nl):
                cols = pl.ds(k0 + j * LANES, LANES)
                o_ref[0, rows, cols] = jnp.where(s_ref[rows, cols] >= thr_r, 0.0, MASK_BIAS).astype(o_ref.dtype)
            return carry

        @pl.when(jnp.logical_not(has_excess))
        def _(write_plain=write_plain):
            lax.fori_loop(0, n_c, write_plain, 0)

        @pl.when(has_excess)
        def _(r0=r0, rows=rows, thr_r=thr_r):
            n_gt = lane_sum(count_acc(lambda a, b: a > b, thr_r, r0, r0 + rb, unrolled=False))
            need_c = (kf - n_gt)[:, 0:1]
            thr_c = thr_r[:, 0:1]
            r = lax.broadcasted_iota(jnp.int32, (tc, tc), 0)
            cidx = lax.broadcasted_iota(jnp.int32, (tc, tc), 1)
            upper = jnp.where(r < cidx, 1.0, 0.0).astype(BF16)

            def write_ties(ci, seen):
                k0 = pl.multiple_of(ci * tc, tc)
                sc = s_ref[rows, pl.ds(k0, tc)]
                eq = sc == thr_c
                before = seen + _dot(jnp.where(eq, 1.0, 0.0).astype(BF16), upper)
                keep = (sc > thr_c) | (eq & (before < need_c))
                o_ref[0, rows, pl.ds(k0, tc)] = jnp.where(keep, 0.0, MASK_BIAS).astype(o_ref.dtype)
                return seen + jnp.sum(jnp.where(eq, 1.0, 0.0), axis=1, keepdims=True)

            lax.fori_loop(0, n_c, write_ties, jnp.zeros((rb, 1), F32))


def _dsa_select(qi, smalls, ki, bias, *, tq, tc, topk):
    b, l, wi = qi.shape
    per_call = tc // tq
    for n_c in range(1, l // tc + 1):
        tile0 = (n_c - 1) * per_call
        bias = pl.pallas_call(
            functools.partial(_dsa_select_body, tq=tq, tc=tc, topk=topk, n_c=n_c, tile0=tile0),
            grid=(b, per_call),
            in_specs=[pl.BlockSpec((1, tq, wi), lambda bi, t, tile0=tile0: (bi, tile0 + t, 0)),
                      pl.BlockSpec((1, tq, LANES), lambda bi, t, tile0=tile0: (bi, tile0 + t, 0)),
                      pl.BlockSpec((1, n_c * tc, IDX_HEAD_DIM), lambda bi, t: (bi, 0, 0)),
                      pl.BlockSpec(memory_space=pl.ANY)],
            out_specs=pl.BlockSpec((1, tq, n_c * tc), lambda bi, t, tile0=tile0: (bi, tile0 + t, 0)),
            out_shape=jax.ShapeDtypeStruct((b, l, l), BF16),
            scratch_shapes=[pltpu.VMEM((tq, n_c * tc), F32)],
            input_output_aliases={3: 0},
            compiler_params=_cparams("parallel", "parallel"),
            name=f"dsa_select_{n_c}",
        )(qi, smalls, ki, bias)
    return bias


def _dsa_attn_body(qt_ref, kt_ref, q_ref, k_ref, v_ref, b_ref, o_ref, m_ref, acc_ref):
    step = pl.program_id(1)
    qi, ki = qt_ref[step], kt_ref[step]
    d = DSA_HEAD_DIM

    def key_tile(first):
        bias = b_ref[0]
        ones = jnp.ones((v_ref.shape[1], d), BF16)
        for h0 in range(0, DSA_HEADS, ATTN_HEAD_GROUP):
            heads = range(h0, h0 + ATTN_HEAD_GROUP)
            s = [_dot_nt(q_ref[0, :, h * d:(h + 1) * d], k_ref[0, :, h * d:(h + 1) * d]).astype(BF16) + bias
                 for h in heads]
            m_new = [jnp.broadcast_to(jnp.max(x, axis=-1, keepdims=True).astype(F32), m_ref.shape[1:]) for x in s]
            if not first:
                m_old = [m_ref[h] for h in heads]
                m_new = [jnp.maximum(mo, mn) for mo, mn in zip(m_old, m_new)]
                alpha = [jnp.exp2(mo - mn) for mo, mn in zip(m_old, m_new)]
            p = [jnp.exp2(x - mn[:, 0:1].astype(BF16)) for x, mn in zip(s, m_new)]
            upd = [_dot(x, jnp.concatenate([v_ref[0, :, h * d:(h + 1) * d], ones], axis=1))
                   for x, h in zip(p, heads)]
            for i, h in enumerate(heads):
                if first:
                    acc_ref[h] = upd[i]
                else:
                    acc_ref[h, :, :d] = alpha[i] * acc_ref[h, :, :d] + upd[i][:, :d]
                    acc_ref[h, :, d:] = alpha[i] * acc_ref[h, :, d:] + upd[i][:, d:]
                m_ref[h] = m_new[i]

    pl.when(ki == 0)(lambda: key_tile(True))
    pl.when(ki > 0)(lambda: key_tile(False))

    @pl.when(ki == qi)
    def _():
        for h in range(DSA_HEADS):
            lo = h * d
            acc = acc_ref[h]
            o_ref[0, :, lo:lo + d] = (acc[:, :d] / acc[:, d:]).astype(o_ref.dtype)


def _dsa_attention(qb, kb, proj, bias, *, t):
    b, l, w = qb.shape
    n = l // t
    pairs = [(i, j) for i in range(n) for j in range(i + 1)]
    q_tile = jnp.asarray([p[0] for p in pairs], jnp.int32)
    k_tile = jnp.asarray([p[1] for p in pairs], jnp.int32)
    v_col = (COL_QKV_B + 2 * DSA_WIDTH) // DSA_WIDTH
    q_idx = lambda bi, s, qt, kt: (bi, qt[s], 0)
    kv_idx = lambda bi, s, qt, kt: (bi, kt[s], 0)
    return pl.pallas_call(
        _dsa_attn_body,
        grid_spec=pltpu.PrefetchScalarGridSpec(
            num_scalar_prefetch=2,
            grid=(b, len(pairs)),
            in_specs=[pl.BlockSpec((1, t, w), q_idx),
                      pl.BlockSpec((1, t, w), kv_idx),
                      pl.BlockSpec((1, t, w), lambda bi, s, qt, kt: (bi, kt[s], v_col)),
                      pl.BlockSpec((1, t, t), lambda bi, s, qt, kt: (bi, qt[s], kt[s]))],
            out_specs=pl.BlockSpec((1, t, w), q_idx),
            scratch_shapes=[pltpu.VMEM((DSA_HEADS, t, LANES), F32),
                            pltpu.VMEM((DSA_HEADS, t, 2 * DSA_HEAD_DIM), F32)]),
        out_shape=jax.ShapeDtypeStruct((b, l, w), BF16),
        compiler_params=_cparams("parallel", "arbitrary"),
        name="dsa_attention",
    )(q_tile, k_tile, qb, kb, proj, bias)


def _rope_tables(seq_len):
    def tab(dim):
        inv_freq = 1.0 / (ROPE_THETA ** (jnp.arange(0, dim, 2, dtype=F32) / dim))
        split = math.gcd(seq_len, ROPE_SPLIT)
        ang_lo = jnp.arange(split, dtype=F32)[None, :, None] * inv_freq
        ang_hi = (jnp.arange(seq_len // split, dtype=F32) * split)[:, None, None] * inv_freq
        c = (jnp.cos(ang_hi) * jnp.cos(ang_lo) - jnp.sin(ang_hi) * jnp.sin(ang_lo)).reshape(seq_len, dim // 2)
        s = (jnp.sin(ang_hi) * jnp.cos(ang_lo) + jnp.cos(ang_hi) * jnp.sin(ang_lo)).reshape(seq_len, dim // 2)
        return jnp.concatenate([c, c], axis=1), jnp.concatenate([-s, s], axis=1)
    ca, sa = tab(DSA_HEAD_DIM)
    c64, s64 = tab(IDX_HEAD_DIM)
    ci = jnp.concatenate([c64, c64], axis=1)
    si = jnp.concatenate([s64, s64], axis=1)
    return ca, sa, ci, si


def _w_in_groups():
    names = ("qkv_a", "z_a", "a", "b", "qkv_b", "q_i", "k_i", "w_i", "g_a", "g_b")
    sizes = (3 * GDN_WIDTH, GDN_WIDTH, GDN_HEADS, GDN_HEADS, 3 * DSA_WIDTH,
             IDX_HEADS * IDX_HEAD_DIM, IDX_HEAD_DIM, IDX_HEADS, D_MODEL, D_MODEL)
    src, start = {}, 0
    for name, size in zip(names, sizes):
        src[name] = (start, size)
        start += size
    main = {"qkv_a": COL_QKV_A, "z_a": COL_Z_A, "qkv_b": COL_QKV_B, "g_a": COL_G_A, "g_b": COL_G_A + D_MODEL,
            "q_i": COL_Q_I}
    smalls = {"a": SM_A, "b": SM_B, "w_i": SM_W, "k_i": SM_K}
    return ([src[n] + (dst,) for n, dst in main.items()], [src[n] + (dst,) for n, dst in smalls.items()], start)


def _regroup_body(w_ref, main_ref, sm_ref):
    main, smalls, _ = _w_in_groups()
    for src, size, dst in main:
        main_ref[:, dst:dst + size] = w_ref[:, src:src + size].astype(main_ref.dtype)
    sm_ref[...] = jnp.zeros_like(sm_ref)
    for src, size, dst in smalls:
        sm_ref[:, dst:dst + size] = w_ref[:, src:src + size]


def _regroup_w_in(w, *, tr):
    depth, d, n = w.shape
    assert n == _w_in_groups()[2]
    return pl.pallas_call(
        _regroup_body,
        grid=(depth, d // tr),
        in_specs=[pl.BlockSpec((None, tr, n), lambda li, i: (li, i, 0))],
        out_specs=[pl.BlockSpec((None, tr, MAIN_COLS), lambda li, i: (li, i, 0)),
                   pl.BlockSpec((None, tr, LANES), lambda li, i: (li, i, 0))],
        out_shape=[jax.ShapeDtypeStruct((depth, d, MAIN_COLS), BF16),
                   jax.ShapeDtypeStruct((depth, d, LANES), F32)],
        compiler_params=_cparams("parallel", "parallel"),
        name="regroup_w_in",
    )(w)


def _cast_body(w_ref, o_ref):
    o_ref[...] = w_ref[...].astype(o_ref.dtype)


def _to_bf16(w, *, tr):
    depth, k, n = w.shape
    spec = pl.BlockSpec((None, tr, n), lambda li, i: (li, i, 0))
    return pl.pallas_call(
        _cast_body,
        grid=(depth, k // tr),
        in_specs=[spec],
        out_specs=spec,
        out_shape=jax.ShapeDtypeStruct(w.shape, BF16),
        compiler_params=_cparams("parallel", "parallel"),
        name="to_bf16",
    )(w)


def kernel(x, norm_mix, w_in, conv_w, a_log, dt_bias, gdn_norm, w_out_gdn, w_out_dsa, w_o,
           norm_ffn, w_gate_up, w_down, norm_final):
    b, l, dm = x.shape
    depth = w_in.shape[0]
    m = b * l
    assert dm == D_MODEL and DSA_HEAD_DIM == LANES and GDN_HEAD_DIM == LANES
    topk = min(TOPK_MAX, l // 4)
    tabs = _rope_tables(l)
    rows, seq = (lambda t: min(t, m)), (lambda t: min(t, l))
    h = x.reshape(m, dm)
    bias = jnp.full((b, l, l), MASK_BIAS, BF16)
    w_main, w_smalls = _regroup_w_in(w_in, tr=min(REGROUP_ROW_TILE, dm))
    w_out_gdn, w_out_dsa, w_o, w_gate_up, w_down = (
        _to_bf16(w, tr=min(REGROUP_ROW_TILE, w.shape[1])) for w in (w_out_gdn, w_out_dsa, w_o, w_gate_up, w_down))
    for layer in range(depth):
        proj = _norm_matmul(h, norm_mix[layer], w_main, layer, tm=rows(ROW_TILE),
                            tn=MAIN_COLS // PROJ_COL_TILES, out_dtype=BF16)
        smalls = _norm_matmul(h, norm_mix[layer], w_smalls, layer, tm=rows(SMALLS_ROW_TILE), tn=LANES,
                              out_dtype=F32, exact=True)
        proj = proj.reshape(b, l, MAIN_COLS)
        smalls = smalls.reshape(b, l, LANES)

        q_a, k_a, v_a = _gdn_prep(proj, conv_w[layer], tl=seq(CONV_TIME_BLOCK))
        o_a = _gdn_chunk(q_a, k_a, v_a, proj, smalls, a_log[layer], dt_bias[layer], gdn_norm[layer],
                         tl=seq(GDN_TIME_BLOCK))

        q_b, k_b, q_i, k_i = _dsa_prep(proj, smalls, tabs, tl=seq(ROPE_TIME_BLOCK))
        bias = _dsa_select(q_i, smalls, k_i, bias, tq=seq(SELECT_Q_TILE), tc=seq(SELECT_KEY_CHUNK), topk=topk)
        o_b = _dsa_attention(q_b, k_b, proj, bias, t=seq(ATTN_TILE))

        h = _mixer_out(o_a.reshape(m, GDN_WIDTH), o_b.reshape(m, DSA_WIDTH), w_out_gdn, w_out_dsa, w_o, layer,
                       proj.reshape(m, MAIN_COLS), h, tm=rows(ROW_TILE))
        last = layer == depth - 1
        h = _ffn(h, norm_ffn[layer], w_gate_up, w_down, layer,
                 tm=rows(ROW_TILE), final_gain=norm_final if last else None)
    return h.reshape(b, l, dm)
```

```python
import functools
import math

import jax
import jax.numpy as jnp
from jax import lax
from jax.experimental import pallas as pl
from jax.experimental.pallas import tpu as pltpu

F32 = jnp.float32
BF16 = jnp.bfloat16

D_MODEL = 1024
GDN_HEADS = 8
GDN_HEAD_DIM = 128
GDN_WIDTH = GDN_HEADS * GDN_HEAD_DIM
CONV_K = 4
GDN_CHUNK = 64
DSA_HEADS = 8
DSA_HEAD_DIM = 128
DSA_WIDTH = DSA_HEADS * DSA_HEAD_DIM
IDX_HEADS = 8
IDX_HEAD_DIM = 64
TOPK_MAX = 256
ROPE_THETA = 10000.0
ROPE_SPLIT = 64
NORM_EPS = 1e-6

LANES = 128
VMEM_LIMIT = 48 * 1024 * 1024
ROW_TILE = 512
SMALLS_ROW_TILE = 1024
CONV_TIME_BLOCK = 256
GDN_TIME_BLOCK = 512
ROPE_TIME_BLOCK = 512
SELECT_Q_TILE = 256
SELECT_KEY_CHUNK = 512
ATTN_TILE = 512
PROJ_COL_TILES = 4
REGROUP_ROW_TILE = 256
BISECT_STEPS = 32
SELECT_ROW_BLOCK = 64
ATTN_HEAD_GROUP = 4
MASK_BIAS = -(2.0 ** 100)
Q_SCALE = DSA_HEAD_DIM ** -0.5 * math.log2(math.e)

COL_QKV_A = 0
COL_Z_A = 3 * GDN_WIDTH
COL_QKV_B = COL_Z_A + GDN_WIDTH
COL_G_A = COL_QKV_B + 3 * DSA_WIDTH
COL_Q_I = COL_G_A + 2 * D_MODEL
MAIN_COLS = COL_Q_I + IDX_HEADS * IDX_HEAD_DIM
SM_A, SM_B, SM_W, SM_K = 0, 8, 16, 64


def _cparams(*sem):
    return pltpu.CompilerParams(dimension_semantics=sem, vmem_limit_bytes=VMEM_LIMIT)


def _rmsnorm_rows(x, gain):
    ms = jnp.mean(x * x, axis=-1, keepdims=True)
    return x * lax.rsqrt(ms + NORM_EPS) * gain


def _dot(a, b):
    return jnp.dot(a, b, preferred_element_type=F32)


def _dot_nt(a, b):
    return lax.dot_general(a, b, (((1,), (1,)), ((), ())), preferred_element_type=F32)


def _resident(shape, layer=None):
    if layer is None:
        return pl.BlockSpec(shape, lambda i, j: (0,) * len(shape), pipeline_mode=pl.Buffered(1))
    return pl.BlockSpec((None,) + shape, lambda i, j: (layer,) + (0,) * len(shape), pipeline_mode=pl.Buffered(1))


def _norm_mm_body(x_ref, g_ref, w_ref, o_ref, xn_ref, *, tn, exact):
    @pl.when(pl.program_id(1) == 0)
    def _():
        xn_ref[...] = _rmsnorm_rows(x_ref[...], g_ref[...]).astype(xn_ref.dtype)

    w = w_ref[:, pl.ds(pl.multiple_of(pl.program_id(1) * tn, tn), tn)]
    if exact:
        x = xn_ref[...]
        x_hi, w_hi = x.astype(BF16), w.astype(BF16)
        x_lo = (x - x_hi.astype(F32)).astype(BF16)
        w_lo = (w - w_hi.astype(F32)).astype(BF16)
        o = _dot(x_hi, w_hi) + (_dot(x_hi, w_lo) + _dot(x_lo, w_hi))
    else:
        o = _dot(xn_ref[...], w)
    o_ref[...] = o.astype(o_ref.dtype)


def _norm_matmul(x, gain, w, layer, *, tm, tn, out_dtype, exact=False):
    m, k = x.shape
    n = w.shape[2]
    return pl.pallas_call(
        functools.partial(_norm_mm_body, tn=tn, exact=exact),
        grid=(m // tm, n // tn),
        in_specs=[pl.BlockSpec((tm, k), lambda i, j: (i, 0)),
                  _resident((1, k)),
                  _resident((k, n), layer)],
        out_specs=pl.BlockSpec((tm, tn), lambda i, j: (i, j)),
        out_shape=jax.ShapeDtypeStruct((m, n), out_dtype),
        scratch_shapes=[pltpu.VMEM((tm, k), w.dtype)],
        compiler_params=_cparams("parallel", "arbitrary"),
        name="norm_matmul_exact" if exact else "norm_matmul",
    )(x, gain.reshape(1, k), w)


def _resident1(shape, layer=None):
    if layer is None:
        return pl.BlockSpec(shape, lambda i: (0,) * len(shape), pipeline_mode=pl.Buffered(1))
    return pl.BlockSpec((None,) + shape, lambda i: (layer,) + (0,) * len(shape), pipeline_mode=pl.Buffered(1))


def _ffn_body(x_ref, g_ref, wgu_ref, wd_ref, *rest, f, final_norm):
    if final_norm:
        gf_ref, o_ref = rest
    else:
        (o_ref,) = rest
    x = x_ref[...]
    xn = _rmsnorm_rows(x, g_ref[...]).astype(BF16)
    g = _dot(xn, wgu_ref[:, :f])
    u = _dot(xn, wgu_ref[:, f:])
    act = (g * jax.nn.sigmoid(g) * u).astype(BF16)
    h = x + _dot(act, wd_ref[...])
    if final_norm:
        h = _rmsnorm_rows(h, gf_ref[...])
    o_ref[...] = h


def _ffn(x, gain, w_gate_up, w_down, layer, *, tm, final_gain=None):
    m, k = x.shape
    f = w_down.shape[1]
    final_norm = final_gain is not None
    in_specs = [pl.BlockSpec((tm, k), lambda i: (i, 0)),
                _resident1((1, k)), _resident1((k, 2 * f), layer), _resident1((f, k), layer)]
    args = [x, gain.reshape(1, k), w_gate_up, w_down]
    if final_norm:
        in_specs.append(_resident1((1, k)))
        args.append(final_gain.reshape(1, k))
    return pl.pallas_call(
        functools.partial(_ffn_body, f=f, final_norm=final_norm),
        grid=(m // tm,),
        in_specs=in_specs,
        out_specs=pl.BlockSpec((tm, k), lambda i: (i, 0)),
        out_shape=jax.ShapeDtypeStruct((m, k), F32),
        compiler_params=_cparams("parallel"),
        name="ffn_norm" if final_norm else "ffn",
    )(*args)


def _mixer_out_body(oa_ref, ob_ref, wa_ref, wb_ref, wo_ref, ga_ref, gb_ref, h_ref, o_ref):
    ya = _dot(oa_ref[...], wa_ref[...])
    yb = _dot(ob_ref[...], wb_ref[...])
    ga = jax.nn.sigmoid(ga_ref[...].astype(F32))
    gb = jax.nn.sigmoid(gb_ref[...].astype(F32))
    merged = (ga * ya + gb * yb).astype(BF16)
    o_ref[...] = h_ref[...] + _dot(merged, wo_ref[...])


def _mixer_out(oa, ob, wa, wb, wo, layer, proj, h, *, tm):
    m, k = oa.shape
    n = wa.shape[2]
    row = lambda w: pl.BlockSpec((tm, w), lambda i: (i, 0))
    return pl.pallas_call(
        _mixer_out_body,
        grid=(m // tm,),
        in_specs=[row(k), row(k), _resident1((k, n), layer), _resident1((k, n), layer), _resident1((n, n), layer),
                  pl.BlockSpec((tm, n), lambda i: (i, COL_G_A // n)),
                  pl.BlockSpec((tm, n), lambda i: (i, COL_G_A // n + 1)),
                  row(n)],
        out_specs=row(n),
        out_shape=jax.ShapeDtypeStruct((m, n), F32),
        compiler_params=_cparams("parallel"),
        name="mixer_out",
    )(oa, ob, wa, wb, wo, proj, proj, h)


HALO = 16
CONV_ROWS = 128


def _gdn_prep_body(x_ref, halo_ref, cw_ref, q_ref, k_ref, v_ref, xs_ref, *, tl):
    i = pl.program_id(1)
    sub = CONV_ROWS
    xs_ref[0:HALO, :] = jnp.where(i == 0, jnp.zeros_like(halo_ref[0]), halo_ref[0])
    xs_ref[HALO:HALO + tl, :] = x_ref[0]
    t = lax.broadcasted_iota(jnp.int32, ((CONV_K - 1) * sub, HALO + sub), 0)
    c = lax.broadcasted_iota(jnp.int32, ((CONV_K - 1) * sub, HALO + sub), 1)
    shift = jnp.where(c == t % sub + HALO - (CONV_K - 1) + t // sub, 1.0, 0.0).astype(BF16)
    for r0 in range(0, tl, sub):
        win = xs_ref[r0:r0 + HALO + sub, :]
        z = _dot(shift, win)
        y = cw_ref[CONV_K - 1:CONV_K, :] * win[HALO:, :].astype(F32)
        for j in range(CONV_K - 1):
            y = y + cw_ref[j:j + 1, :] * z[j * sub:(j + 1) * sub]
        y = y * jax.nn.sigmoid(y)
        rows = slice(r0, r0 + sub)
        for h in range(GDN_HEADS):
            lo = h * GDN_HEAD_DIM
            qh = y[:, lo:lo + GDN_HEAD_DIM]
            kh = y[:, GDN_WIDTH + lo:GDN_WIDTH + lo + GDN_HEAD_DIM]
            qn = qh * lax.rsqrt(jnp.sum(qh * qh, axis=-1, keepdims=True) + NORM_EPS)
            kn = kh * lax.rsqrt(jnp.sum(kh * kh, axis=-1, keepdims=True) + NORM_EPS)
            q_ref[0, rows, lo:lo + GDN_HEAD_DIM] = (qn * (GDN_HEAD_DIM ** -0.5)).astype(q_ref.dtype)
            k_ref[0, rows, lo:lo + GDN_HEAD_DIM] = kn.astype(k_ref.dtype)
        v_ref[0, rows, :] = y[:, 2 * GDN_WIDTH:].astype(v_ref.dtype)


def _gdn_prep(proj, conv_w, *, tl):
    b, l, _ = proj.shape
    c = 3 * GDN_WIDTH
    r = tl // HALO
    out = jax.ShapeDtypeStruct((b, l, GDN_WIDTH), BF16)
    ospec = pl.BlockSpec((1, tl, GDN_WIDTH), lambda bi, i: (bi, i, 0))
    return pl.pallas_call(
        functools.partial(_gdn_prep_body, tl=tl),
        grid=(b, l // tl),
        in_specs=[pl.BlockSpec((1, tl, c), lambda bi, i: (bi, i, 0)),
                  pl.BlockSpec((1, HALO, c), lambda bi, i: (bi, jnp.maximum(i * r - 1, 0), 0)),
                  pl.BlockSpec((CONV_K, c), lambda bi, i: (0, 0))],
        out_specs=[ospec, ospec, ospec],
        out_shape=[out, out, out],
        scratch_shapes=[pltpu.VMEM((HALO + tl, c), BF16)],
        compiler_params=_cparams("parallel", "parallel"),
        name="gdn_prep",
    )(proj, proj, conv_w)


def _gdn_chunk_body(q_ref, k_ref, v_ref, z_ref, sm_ref, alog_ref, dtb_ref, gn_ref, o_ref,
                    s_ref, wq_ref, qkk_ref, u_ref, gl_ref, *, nch, nb):
    c = GDN_CHUNK
    d = GDN_HEAD_DIM
    probs = [(bb, h) for bb in range(nb) for h in range(GDN_HEADS)]

    @pl.when(pl.program_id(1) == 0)
    def _():
        s_ref[...] = jnp.zeros_like(s_ref)

    row = lax.broadcasted_iota(jnp.int32, (c, c), 0)
    col = lax.broadcasted_iota(jnp.int32, (c, c), 1)
    tril_incl = row >= col
    tril_strict = row > col
    eye = (row == col).astype(F32)
    row_id = lax.broadcasted_iota(jnp.int32, (c, LANES), 0)

    def intra(ci, carry):
        r0 = pl.multiple_of(ci * c, c)
        gam_col, gam_row, beta_all = [], [], []
        for bb in range(nb):
            sm = sm_ref[bb, pl.ds(r0, c), :]
            g_all = -jnp.exp(alog_ref[...]) * jax.nn.softplus(sm + dtb_ref[...])
            beta_all.append(jax.nn.sigmoid(sm))
            gam = g_all
            step = 1
            while step < c:
                gam = gam + jnp.where(row_id >= step, pltpu.roll(gam, step, axis=0), 0.0)
                step *= 2
            gam_col.append(gam)
            gam_row.append(gam.T)
        gc = [gam_col[bb][:, SM_A + h:SM_A + h + 1] for bb, h in probs]
        bc = [beta_all[bb][:, SM_B + h:SM_B + h + 1] for bb, h in probs]
        g_last = [g[c - 1:c, :] for g in gc]
        decay = [jnp.exp(jnp.minimum(g - gam_row[bb][SM_A + h:SM_A + h + 1, :], 0.0))
                 for g, (bb, h) in zip(gc, probs)]
        qh = [q_ref[bb, pl.ds(r0, c), h * d:(h + 1) * d] for bb, h in probs]
        kh = [k_ref[bb, pl.ds(r0, c), h * d:(h + 1) * d] for bb, h in probs]
        vh = [v_ref[bb, pl.ds(r0, c), h * d:(h + 1) * d] for bb, h in probs]
        kq = [_dot_nt(jnp.concatenate([k, q], axis=0), k) for k, q in zip(kh, qh)]
        qk = [jnp.where(tril_incl, x[c:] * dc, 0.0) for x, dc in zip(kq, decay)]
        n_pow = [-jnp.where(tril_strict, b_ * x[:c] * dc, 0.0) for x, b_, dc in zip(kq, bc, decay)]
        t_inv = [eye + n for n in n_pow]
        n_pow = [_dot(n.astype(BF16), n.astype(BF16)) for n in n_pow]
        for _ in range(int(math.log2(c)) - 2):
            prod = [_dot(jnp.concatenate([n, t], axis=0).astype(BF16), n.astype(BF16))
                    for n, t in zip(n_pow, t_inv)]
            n_pow = [p[:c] for p in prod]
            t_inv = [t + p[c:] for t, p in zip(t_inv, prod)]
        t_inv = [t + _dot(t.astype(BF16), n.astype(BF16)) for n, t in zip(n_pow, t_inv)]
        e_gc = [jnp.exp(g) for g in gc]
        kf = [k.astype(F32) for k in kh]
        rhs = [jnp.concatenate([v.astype(F32) * b_, k * (b_ * e)], axis=1)
               for v, k, b_, e in zip(vh, kf, bc, e_gc)]
        uw = [_dot(t.astype(BF16), r.astype(BF16)) for t, r in zip(t_inv, rhs)]
        for i, (bb, h) in enumerate(probs):
            k_dec = kf[i] * jnp.exp(g_last[i] - gc[i])
            u_ref[bb, ci, h] = uw[i][:, :d]
            wq_ref[bb, ci, h] = jnp.concatenate([uw[i][:, d:], qh[i].astype(F32) * e_gc[i]], axis=0).astype(BF16)
            qkk_ref[bb, ci, h] = jnp.concatenate([qk[i], k_dec.T], axis=0).astype(BF16)
            gl_ref[bb, ci, h] = jnp.broadcast_to(jnp.exp(g_last[i]), (1, d))
        return carry

    lax.fori_loop(0, nch, intra, 0)

    def scan(ci, carry):
        r0 = pl.multiple_of(ci * c, c)
        s_old = [s_ref[bb, h] for bb, h in probs]
        m1 = [_dot(wq_ref[bb, ci, h], s.astype(BF16)) for s, (bb, h) in zip(s_old, probs)]
        v_new = [u_ref[bb, ci, h] - m[:c] for m, (bb, h) in zip(m1, probs)]
        m2 = [_dot(qkk_ref[bb, ci, h], v.astype(BF16)) for v, (bb, h) in zip(v_new, probs)]
        for i, (bb, h) in enumerate(probs):
            s_ref[bb, h] = s_old[i] * gl_ref[bb, ci, h] + m2[i][c:]
            o = m1[i][c:] + m2[i][:c]
            o = o * lax.rsqrt(jnp.mean(o * o, axis=-1, keepdims=True) + NORM_EPS) * gn_ref[...]
            z = z_ref[bb, pl.ds(r0, c), h * d:(h + 1) * d].astype(F32)
            o_ref[bb, pl.ds(r0, c), h * d:(h + 1) * d] = (o * (z * jax.nn.sigmoid(z))).astype(o_ref.dtype)
        return carry

    lax.fori_loop(0, nch, scan, 0)


def _gdn_chunk(q, k, v, proj, smalls, a_log, dt_bias, gdn_norm, *, tl):
    b, l, _ = q.shape
    c, d, nh = GDN_CHUNK, GDN_HEAD_DIM, GDN_HEADS
    nch = tl // c
    nb = 2 if b % 2 == 0 else 1
    pad = lambda t, at: jnp.zeros((1, LANES), F32).at[0, at:at + t.shape[0]].set(t.astype(F32))
    qspec = pl.BlockSpec((nb, tl, GDN_WIDTH), lambda bi, i: (bi, i, 0))
    vec = pl.BlockSpec((1, LANES), lambda bi, i: (0, 0))
    return pl.pallas_call(
        functools.partial(_gdn_chunk_body, nch=nch, nb=nb),
        grid=(b // nb, l // tl),
        in_specs=[qspec, qspec, qspec,
                  pl.BlockSpec((nb, tl, GDN_WIDTH), lambda bi, i: (bi, i, COL_Z_A // GDN_WIDTH)),
                  pl.BlockSpec((nb, tl, LANES), lambda bi, i: (bi, i, 0)),
                  vec, vec, vec],
        out_specs=qspec,
        out_shape=jax.ShapeDtypeStruct((b, l, GDN_WIDTH), BF16),
        scratch_shapes=[pltpu.VMEM((nb, nh, d, d), F32),
                        pltpu.VMEM((nb, nch, nh, 2 * c, d), BF16),
                        pltpu.VMEM((nb, nch, nh, c + d, c), BF16),
                        pltpu.VMEM((nb, nch, nh, c, d), F32),
                        pltpu.VMEM((nb, nch, nh, 1, d), F32)],
        compiler_params=_cparams("parallel", "arbitrary"),
        name="gdn_chunk",
    )(q, k, v, proj, smalls, pad(a_log, SM_A), pad(dt_bias, SM_A), gdn_norm.reshape(1, LANES).astype(F32))


def _swap_halves(x, half):
    if 2 * half == LANES:
        return pltpu.roll(x, half, axis=1)
    lane = lax.broadcasted_iota(jnp.int32, x.shape, 1)
    first = (lane % (2 * half)) < half
    return jnp.where(first, pltpu.roll(x, LANES - half, axis=1), pltpu.roll(x, half, axis=1))


def _dsa_prep_body(qk_ref, qi_ref, sm_ref, ca_ref, sa_ref, ci_ref, si_ref,
                   qb_ref, kb_ref, qio_ref, kio_ref):
    ca, sa = ca_ref[...], sa_ref[...]
    for h in range(2 * DSA_HEADS):
        lo = h * DSA_HEAD_DIM
        x = qk_ref[0, :, lo:lo + DSA_HEAD_DIM].astype(F32)
        y = x * ca + _swap_halves(x, DSA_HEAD_DIM // 2) * sa
        if h < DSA_HEADS:
            qb_ref[0, :, lo:lo + DSA_HEAD_DIM] = (y * Q_SCALE).astype(BF16)
        else:
            lo -= DSA_WIDTH
            kb_ref[0, :, lo:lo + DSA_HEAD_DIM] = y.astype(BF16)
    ci, si = ci_ref[...], si_ref[...]
    for p in range(IDX_HEADS * IDX_HEAD_DIM // LANES):
        lo = p * LANES
        x = qi_ref[0, :, lo:lo + LANES].astype(F32)
        y = x * ci + _swap_halves(x, IDX_HEAD_DIM // 2) * si
        qio_ref[0, :, lo:lo + LANES] = y.astype(BF16)
    x = sm_ref[0]
    y = x * ci + _swap_halves(x, IDX_HEAD_DIM // 2) * si
    kio_ref[0] = (y[:, SM_K:] * (IDX_HEAD_DIM ** -0.5)).astype(BF16)


def _dsa_prep(proj, smalls, tabs, *, tl):
    b, l, _ = proj.shape
    w2 = 2 * DSA_WIDTH
    wi = IDX_HEADS * IDX_HEAD_DIM
    tab = pl.BlockSpec((tl, LANES), lambda bi, i: (i, 0))
    big = lambda n: pl.BlockSpec((1, tl, n), lambda bi, i: (bi, i, 0))
    return pl.pallas_call(
        _dsa_prep_body,
        grid=(b, l // tl),
        in_specs=[pl.BlockSpec((1, tl, w2), lambda bi, i: (bi, i, COL_QKV_B // w2)),
                  pl.BlockSpec((1, tl, wi), lambda bi, i: (bi, i, COL_Q_I // wi)),
                  big(LANES), tab, tab, tab, tab],
        out_specs=[big(DSA_WIDTH), big(DSA_WIDTH), big(wi), big(IDX_HEAD_DIM)],
        out_shape=[jax.ShapeDtypeStruct((b, l, DSA_WIDTH), BF16),
                   jax.ShapeDtypeStruct((b, l, DSA_WIDTH), BF16),
                   jax.ShapeDtypeStruct((b, l, wi), BF16),
                   jax.ShapeDtypeStruct((b, l, IDX_HEAD_DIM), BF16)],
        compiler_params=_cparams("parallel", "parallel"),
        name="dsa_prep",
    )(proj, proj, smalls, *tabs)


def _dsa_select_body(qi_ref, sm_ref, ki_ref, bias_in_ref, o_ref, s_ref, *, tq, tc, topk, n_c, tile0):
    del bias_in_ref
    nl = tc // LANES
    half = tq // 2
    q0 = (tile0 + pl.program_id(1)) * tq
    kf = float(topk)
    w_all = sm_ref[0] * (IDX_HEADS ** -0.5)
    rb = min(SELECT_ROW_BLOCK, half)
    rep = lambda col: jnp.broadcast_to(col, (col.shape[0], LANES))
    lane_tiles = lambda x: [x[:, j * LANES:(j + 1) * LANES] for j in range(nl)]

    def score_chunk(k0, carry, last):
        mn, mx, n_pos, n_nonneg = carry
        kc = ki_ref[0, pl.ds(k0, tc), :]
        s = jnp.zeros((tq, tc), F32)
        for h in range(IDX_HEADS):
            qh = qi_ref[0, :, h * IDX_HEAD_DIM:(h + 1) * IDX_HEAD_DIM]
            s = s + w_all[:, SM_W + h:SM_W + h + 1] * jnp.maximum(_dot_nt(qh, kc), 0.0)
        if last:
            qpos = q0 + lax.broadcasted_iota(jnp.int32, (tq, tc), 0)
            causal = k0 + lax.broadcasted_iota(jnp.int32, (tq, tc), 1) <= qpos
            s_lo, s_hi = jnp.where(causal, s, -jnp.inf), jnp.where(causal, s, jnp.inf)
        else:
            s_lo = s_hi = s
        s_ref[:, pl.ds(k0, tc)] = s_lo
        for lo_t, hi_t in zip(lane_tiles(s_lo), lane_tiles(s_hi)):
            mx = jnp.maximum(mx, lo_t)
            mn = jnp.minimum(mn, hi_t)
            n_pos = n_pos + jnp.where(lo_t > 0.0, 1.0, 0.0)
            n_nonneg = n_nonneg + jnp.where(lo_t >= 0.0, 1.0, 0.0)
        return mn, mx, n_pos, n_nonneg

    zeros = jnp.zeros((tq, LANES), F32)
    stats = lax.fori_loop(0, n_c - 1, lambda ci, c: score_chunk(pl.multiple_of(ci * tc, tc), c, False),
                          (zeros + jnp.inf, zeros - jnp.inf, zeros, zeros), unroll=2)
    mn, mx, n_pos, n_nonneg = score_chunk((n_c - 1) * tc, stats, True)
    row_min = rep(jnp.min(mn, axis=1, keepdims=True))
    row_max = rep(jnp.max(mx, axis=1, keepdims=True))
    n_pos = rep(jnp.sum(n_pos, axis=1, keepdims=True))
    n_nonneg = rep(jnp.sum(n_nonneg, axis=1, keepdims=True))

    def reduce_keys(fn, init, combine, r_lo, r_hi, unrolled):
        accs = []
        for r0 in range(r_lo, r_hi, rb):
            def chunk(k0, acc, r0=r0):
                for j in range(nl):
                    acc = combine(acc, fn(s_ref[r0:r0 + rb, pl.ds(k0 + j * LANES, LANES)], r0 - r_lo))
                return acc
            acc = jnp.full((rb, LANES), init, F32)
            if unrolled:
                for ci in range(n_c):
                    acc = chunk(ci * tc, acc)
            else:
                acc = lax.fori_loop(0, n_c, lambda ci, a: chunk(pl.multiple_of(ci * tc, tc), a), acc)
            accs.append(acc)
        return jnp.concatenate(accs, axis=0)

    def count_acc(op, th, r_lo, r_hi, unrolled=True):
        return reduce_keys(lambda t, r: jnp.where(op(t, th[r:r + rb]), 1.0, 0.0), 0.0, jnp.add, r_lo, r_hi, unrolled)

    lane_max = lambda x: rep(jnp.max(x, axis=1, keepdims=True))
    lane_sum = lambda x: rep(jnp.sum(x, axis=1, keepdims=True))
    ge = lambda a, b: a >= b
    count = lambda op, th: lane_sum(count_acc(op, th, 0, tq, unrolled=False))
    is_open = lambda lo, hi: jnp.max(hi - lo) > 0.0
    middle = lambda lo, hi: lo + (hi - lo) * 0.5

    def narrow(lo, hi, mid, cnt):
        hit = cnt == kf
        above = cnt > kf
        return jnp.where(hit | above, mid, lo), jnp.where(hit | jnp.logical_not(above), mid, hi)

    n_causal = (q0 + 1 + lax.broadcasted_iota(jnp.int32, (tq, LANES), 0)).astype(F32)
    few = n_causal <= kf
    zero_tied = (n_pos < kf) & (n_nonneg >= kf)
    starts_closed = few | zero_tied
    closed_at = jnp.where(few, -jnp.finfo(F32).max, 0.0)
    positive = n_pos >= kf
    lo = jnp.where(starts_closed, closed_at, jnp.where(positive, jnp.maximum(row_min, 0.0), row_min))
    hi = jnp.where(starts_closed, closed_at, jnp.where(positive, row_max, jnp.minimum(row_max, 0.0)))
    surplus = jnp.where((zero_tied & (n_nonneg > kf)) | (jnp.logical_not(starts_closed) & (lo >= hi)), 1.0, 0.0)

    def search_cond(c):
        return (c[0] < BISECT_STEPS) & (c[7] > 0)

    def search_body(c):
        it, lo_a, hi_a, lo_b, hi_b, acc_b, mid_b, _ = c
        go_on = (is_open(lo_a, hi_a) | is_open(lo_b, hi_b)).astype(jnp.int32)
        mid_a = middle(lo_a, hi_a)
        acc_a = count_acc(ge, mid_a, 0, half)
        lo_b, hi_b = narrow(lo_b, hi_b, mid_b, lane_sum(acc_b))
        mid_b = middle(lo_b, hi_b)
        acc_b = count_acc(ge, mid_b, half, tq)
        lo_a, hi_a = narrow(lo_a, hi_a, mid_a, lane_sum(acc_a))
        return it + 1, lo_a, hi_a, lo_b, hi_b, acc_b, mid_b, go_on

    lo_a, hi_a, lo_b, hi_b = lo[:half], hi[:half], lo[half:], hi[half:]
    mid_b = middle(lo_b, hi_b)
    _, lo_a, hi_a, lo_b, hi_b, acc_b, mid_b, _ = lax.while_loop(
        search_cond, search_body,
        (jnp.int32(0), lo_a, hi_a, lo_b, hi_b, count_acc(ge, mid_b, half, tq), mid_b, jnp.int32(1)))
    lo_b, hi_b = narrow(lo_b, hi_b, mid_b, lane_sum(acc_b))
    lo = jnp.concatenate([lo_a, lo_b], axis=0)
    hi = jnp.concatenate([hi_a, hi_b], axis=0)

    def peel_cond(c):
        return (c[0] < topk + 2) & (c[4] > 0)

    def peel_body(c):
        it, lo, hi, surplus, _ = c
        still = hi > lo
        widen = jnp.where(it == 0, jnp.finfo(F32).max, 0.0)
        below = jnp.where(hi == row_max, hi + widen, hi)
        top = lane_max(reduce_keys(
            lambda t, r: jnp.where((t >= lo[r:r + rb]) & (t < below[r:r + rb]), t, -jnp.inf),
            -jnp.inf, jnp.maximum, 0, tq, False))
        n_top = count(ge, top)
        found = still & (n_top >= kf)
        lo = jnp.where(found, top, lo)
        hi = jnp.where(still, top, hi)
        surplus = jnp.where(found & (n_top > kf), 1.0, surplus)
        return it + 1, lo, hi, surplus, is_open(lo, hi).astype(jnp.int32)

    _, thr, _, surplus, _ = lax.while_loop(
        peel_cond, peel_body, (jnp.int32(0), lo, hi, surplus, is_open(lo, hi).astype(jnp.int32)))

    for r0 in range(0, tq, rb):
        rows = slice(r0, r0 + rb)
        thr_r = thr[rows]
        has_excess = jnp.max(surplus[rows]) > 0.0

        def write_plain(ci, carry, rows=rows, thr_r=thr_r):
            k0 = pl.multiple_of(ci * tc, tc)
            for j in range(nl):
                cols = pl.ds(k0 + j * LANES, LANES)
                o_ref[0, rows, cols] = jnp.where(s_ref[rows, cols] >= thr_r, 0.0, MASK_BIAS).astype(o_ref.dtype)
            return carry

        @pl.when(jnp.logical_not(has_excess))
        def _(write_plain=write_plain):
            lax.fori_loop(0, n_c, write_plain, 0)

        @pl.when(has_excess)
        def _(r0=r0, rows=rows, thr_r=thr_r):
            n_gt = lane_sum(count_acc(lambda a, b: a > b, thr_r, r0, r0 + rb, unrolled=False))
            need_c = (kf - n_gt)[:, 0:1]
            thr_c = thr_r[:, 0:1]
            r = lax.broadcasted_iota(jnp.int32, (tc, tc), 0)
            cidx = lax.broadcasted_iota(jnp.int32, (tc, tc), 1)
            upper = jnp.where(r < cidx, 1.0, 0.0).astype(BF16)

            def write_ties(ci, seen):
                k0 = pl.multiple_of(ci * tc, tc)
                sc = s_ref[rows, pl.ds(k0, tc)]
                eq = sc == thr_c
                before = seen + _dot(jnp.where(eq, 1.0, 0.0).astype(BF16), upper)
                keep = (sc > thr_c) | (eq & (before < need_c))
                o_ref[0, rows, pl.ds(k0, tc)] = jnp.where(keep, 0.0, MASK_BIAS).astype(o_ref.dtype)
                return seen + jnp.sum(jnp.where(eq, 1.0, 0.0), axis=1, keepdims=True)

            lax.fori_loop(0, n_c, write_ties, jnp.zeros((rb, 1), F32))


def _dsa_select(qi, smalls, ki, bias, *, tq, tc, topk):
    b, l, wi = qi.shape
    per_call = tc // tq
    for n_c in range(1, l // tc + 1):
        tile0 = (n_c - 1) * per_call
        bias = pl.pallas_call(
            functools.partial(_dsa_select_body, tq=tq, tc=tc, topk=topk, n_c=n_c, tile0=tile0),
            grid=(b, per_call),
            in_specs=[pl.BlockSpec((1, tq, wi), lambda bi, t, tile0=tile0: (bi, tile0 + t, 0)),
                      pl.BlockSpec((1, tq, LANES), lambda bi, t, tile0=tile0: (bi, tile0 + t, 0)),
                      pl.BlockSpec((1, n_c * tc, IDX_HEAD_DIM), lambda bi, t: (bi, 0, 0)),
                      pl.BlockSpec(memory_space=pl.ANY)],
            out_specs=pl.BlockSpec((1, tq, n_c * tc), lambda bi, t, tile0=tile0: (bi, tile0 + t, 0)),
            out_shape=jax.ShapeDtypeStruct((b, l, l), BF16),
            scratch_shapes=[pltpu.VMEM((tq, n_c * tc), F32)],
            input_output_aliases={3: 0},
            compiler_params=_cparams("parallel", "parallel"),
            name=f"dsa_select_{n_c}",
        )(qi, smalls, ki, bias)
    return bias


def _mask_fill_body(o_ref):
    o_ref[...] = jnp.full(o_ref.shape, MASK_BIAS, o_ref.dtype)


def _masked_bias(b, l, *, tr):
    return pl.pallas_call(
        _mask_fill_body,
        grid=(b, l // tr),
        out_specs=pl.BlockSpec((1, tr, l), lambda bi, i: (bi, i, 0)),
        out_shape=jax.ShapeDtypeStruct((b, l, l), BF16),
        compiler_params=_cparams("parallel", "parallel"),
        name="masked_bias",
    )()


def _dsa_attn_body(qt_ref, kt_ref, q_ref, k_ref, v_ref, b_ref, o_ref, m_ref, acc_ref):
    step = pl.program_id(1)
    qi, ki = qt_ref[step], kt_ref[step]
    d = DSA_HEAD_DIM

    def key_tile(first):
        bias = b_ref[0]
        ones = jnp.ones((v_ref.shape[1], d), BF16)
        for h0 in range(0, DSA_HEADS, ATTN_HEAD_GROUP):
            heads = range(h0, h0 + ATTN_HEAD_GROUP)
            s = [_dot_nt(q_ref[0, :, h * d:(h + 1) * d], k_ref[0, :, h * d:(h + 1) * d]).astype(BF16) + bias
                 for h in heads]
            m_new = [jnp.broadcast_to(jnp.max(x, axis=-1, keepdims=True).astype(F32), m_ref.shape[1:]) for x in s]
            if not first:
                m_old = [m_ref[h] for h in heads]
                m_new = [jnp.maximum(mo, mn) for mo, mn in zip(m_old, m_new)]
                alpha = [jnp.exp2(mo - mn) for mo, mn in zip(m_old, m_new)]
            p = [jnp.exp2(x - mn[:, 0:1].astype(BF16)) for x, mn in zip(s, m_new)]
            upd = [_dot(x, jnp.concatenate([v_ref[0, :, h * d:(h + 1) * d], ones], axis=1))
                   for x, h in zip(p, heads)]
            for i, h in enumerate(heads):
                if first:
                    acc_ref[h] = upd[i]
                else:
                    acc_ref[h, :, :d] = alpha[i] * acc_ref[h, :, :d] + upd[i][:, :d]
                    acc_ref[h, :, d:] = alpha[i] * acc_ref[h, :, d:] + upd[i][:, d:]
                m_ref[h] = m_new[i]

    pl.when(ki == 0)(lambda: key_tile(True))
    pl.when(ki > 0)(lambda: key_tile(False))

    @pl.when(ki == qi)
    def _():
        for h in range(DSA_HEADS):
            lo = h * d
            acc = acc_ref[h]
            o_ref[0, :, lo:lo + d] = (acc[:, :d] / acc[:, d:]).astype(o_ref.dtype)


def _dsa_attention(qb, kb, proj, bias, *, t):
    b, l, w = qb.shape
    n = l // t
    pairs = [(i, j) for i in range(n) for j in range(i + 1)]
    q_tile = jnp.asarray([p[0] for p in pairs], jnp.int32)
    k_tile = jnp.asarray([p[1] for p in pairs], jnp.int32)
    v_col = (COL_QKV_B + 2 * DSA_WIDTH) // DSA_WIDTH
    q_idx = lambda bi, s, qt, kt: (bi, qt[s], 0)
    kv_idx = lambda bi, s, qt, kt: (bi, kt[s], 0)
    return pl.pallas_call(
        _dsa_attn_body,
        grid_spec=pltpu.PrefetchScalarGridSpec(
            num_scalar_prefetch=2,
            grid=(b, len(pairs)),
            in_specs=[pl.BlockSpec((1, t, w), q_idx),
                      pl.BlockSpec((1, t, w), kv_idx),
                      pl.BlockSpec((1, t, w), lambda bi, s, qt, kt: (bi, kt[s], v_col)),
                      pl.BlockSpec((1, t, t), lambda bi, s, qt, kt: (bi, qt[s], kt[s]))],
            out_specs=pl.BlockSpec((1, t, w), q_idx),
            scratch_shapes=[pltpu.VMEM((DSA_HEADS, t, LANES), F32),
                            pltpu.VMEM((DSA_HEADS, t, 2 * DSA_HEAD_DIM), F32)]),
        out_shape=jax.ShapeDtypeStruct((b, l, w), BF16),
        compiler_params=_cparams("parallel", "arbitrary"),
        name="dsa_attention",
    )(q_tile, k_tile, qb, kb, proj, bias)


def _rope_tables(seq_len):
    def tab(dim):
        inv_freq = 1.0 / (ROPE_THETA ** (jnp.arange(0, dim, 2, dtype=F32) / dim))
        split = math.gcd(seq_len, ROPE_SPLIT)
        ang_lo = jnp.arange(split, dtype=F32)[None, :, None] * inv_freq
        ang_hi = (jnp.arange(seq_len // split, dtype=F32) * split)[:, None, None] * inv_freq
        c = (jnp.cos(ang_hi) * jnp.cos(ang_lo) - jnp.sin(ang_hi) * jnp.sin(ang_lo)).reshape(seq_len, dim // 2)
        s = (jnp.sin(ang_hi) * jnp.cos(ang_lo) + jnp.cos(ang_hi) * jnp.sin(ang_lo)).reshape(seq_len, dim // 2)
        return jnp.concatenate([c, c], axis=1), jnp.concatenate([-s, s], axis=1)
    ca, sa = tab(DSA_HEAD_DIM)
    c64, s64 = tab(IDX_HEAD_DIM)
    ci = jnp.concatenate([c64, c64], axis=1)
    si = jnp.concatenate([s64, s64], axis=1)
    return ca, sa, ci, si


def _w_in_groups():
    names = ("qkv_a", "z_a", "a", "b", "qkv_b", "q_i", "k_i", "w_i", "g_a", "g_b")
    sizes = (3 * GDN_WIDTH, GDN_WIDTH, GDN_HEADS, GDN_HEADS, 3 * DSA_WIDTH,
             IDX_HEADS * IDX_HEAD_DIM, IDX_HEAD_DIM, IDX_HEADS, D_MODEL, D_MODEL)
    src, start = {}, 0
    for name, size in zip(names, sizes):
        src[name] = (start, size)
        start += size
    main = {"qkv_a": COL_QKV_A, "z_a": COL_Z_A, "qkv_b": COL_QKV_B, "g_a": COL_G_A, "g_b": COL_G_A + D_MODEL,
            "q_i": COL_Q_I}
    smalls = {"a": SM_A, "b": SM_B, "w_i": SM_W, "k_i": SM_K}
    return ([src[n] + (dst,) for n, dst in main.items()], [src[n] + (dst,) for n, dst in smalls.items()], start)


def _regroup_body(w_ref, main_ref, sm_ref):
    main, smalls, _ = _w_in_groups()
    for src, size, dst in main:
        main_ref[:, dst:dst + size] = w_ref[:, src:src + size].astype(main_ref.dtype)
    sm_ref[...] = jnp.zeros_like(sm_ref)
    for src, size, dst in smalls:
        sm_ref[:, dst:dst + size] = w_ref[:, src:src + size]


def _regroup_w_in(w, *, tr):
    depth, d, n = w.shape
    assert n == _w_in_groups()[2]
    return pl.pallas_call(
        _regroup_body,
        grid=(depth, d // tr),
        in_specs=[pl.BlockSpec((None, tr, n), lambda li, i: (li, i, 0))],
        out_specs=[pl.BlockSpec((None, tr, MAIN_COLS), lambda li, i: (li, i, 0)),
                   pl.BlockSpec((None, tr, LANES), lambda li, i: (li, i, 0))],
        out_shape=[jax.ShapeDtypeStruct((depth, d, MAIN_COLS), BF16),
                   jax.ShapeDtypeStruct((depth, d, LANES), F32)],
        compiler_params=_cparams("parallel", "parallel"),
        name="regroup_w_in",
    )(w)


def _cast_body(w_ref, o_ref):
    o_ref[...] = w_ref[...].astype(o_ref.dtype)


def _to_bf16(w, *, tr):
    depth, k, n = w.shape
    spec = pl.BlockSpec((None, tr, n), lambda li, i: (li, i, 0))
    return pl.pallas_call(
        _cast_body,
        grid=(depth, k // tr),
        in_specs=[spec],
        out_specs=spec,
        out_shape=jax.ShapeDtypeStruct(w.shape, BF16),
        compiler_params=_cparams("parallel", "parallel"),
        name="to_bf16",
    )(w)


def kernel(x, norm_mix, w_in, conv_w, a_log, dt_bias, gdn_norm, w_out_gdn, w_out_dsa, w_o,
           norm_ffn, w_gate_up, w_down, norm_final):
    b, l, dm = x.shape
    depth = w_in.shape[0]
    m = b * l
    assert dm == D_MODEL and DSA_HEAD_DIM == LANES and GDN_HEAD_DIM == LANES
    topk = min(TOPK_MAX, l // 4)
    tabs = _rope_tables(l)
    rows, seq = (lambda t: min(t, m)), (lambda t: min(t, l))
    h = x.reshape(m, dm)
    bias = _masked_bias(b, l, tr=seq(ATTN_TILE))
    w_main, w_smalls = _regroup_w_in(w_in, tr=min(REGROUP_ROW_TILE, dm))
    w_out_gdn, w_out_dsa, w_o, w_gate_up, w_down = (
        _to_bf16(w, tr=min(REGROUP_ROW_TILE, w.shape[1])) for w in (w_out_gdn, w_out_dsa, w_o, w_gate_up, w_down))
    for layer in range(depth):
        proj = _norm_matmul(h, norm_mix[layer], w_main, layer, tm=rows(ROW_TILE),
                            tn=MAIN_COLS // PROJ_COL_TILES, out_dtype=BF16)
        smalls = _norm_matmul(h, norm_mix[layer], w_smalls, layer, tm=rows(SMALLS_ROW_TILE), tn=LANES,
                              out_dtype=F32, exact=True)
        proj = proj.reshape(b, l, MAIN_COLS)
        smalls = smalls.reshape(b, l, LANES)

        q_a, k_a, v_a = _gdn_prep(proj, conv_w[layer], tl=seq(CONV_TIME_BLOCK))
        o_a = _gdn_chunk(q_a, k_a, v_a, proj, smalls, a_log[layer], dt_bias[layer], gdn_norm[layer],
                         tl=seq(GDN_TIME_BLOCK))

        q_b, k_b, q_i, k_i = _dsa_prep(proj, smalls, tabs, tl=seq(ROPE_TIME_BLOCK))
        bias = _dsa_select(q_i, smalls, k_i, bias, tq=seq(SELECT_Q_TILE), tc=seq(SELECT_KEY_CHUNK), topk=topk)
        o_b = _dsa_attention(q_b, k_b, proj, bias, t=seq(ATTN_TILE))

        h = _mixer_out(o_a.reshape(m, GDN_WIDTH), o_b.reshape(m, DSA_WIDTH), w_out_gdn, w_out_dsa, w_o, layer,
                       proj.reshape(m, MAIN_COLS), h, tm=rows(ROW_TILE))
        last = layer == depth - 1
        h = _ffn(h, norm_ffn[layer], w_gate_up, w_down, layer,
                 tm=rows(ROW_TILE), final_gain=norm_final if last else None)
    return h.reshape(b, l, dm)
```
